```python
import math
import jax, jax.numpy as jnp
from jax import lax
import numpy as np

D_MODEL = 2048
BATCH = 2
SEQ = 16384
DEPTH = 2

HEAD_DIM = 128
ROPE_THETA = 10000.0
NORM_EPS = 1e-6
BLOCK_Q = 128
D_FF = 1536

A_HEADS = 6
A_PATTERNS = ((128, 1), (512, 4), (2048, 16))
B_HEADS = 4
B_Q_LORA = 384
B_KV_LORA = 256
B_NOPE = 64
B_ROPE = 32
B_V = 128
D_MIX_EVEN = A_HEADS * HEAD_DIM + B_HEADS * B_V
EVEN_SPLITS = (A_HEADS * HEAD_DIM, A_HEADS * HEAD_DIM, A_HEADS * HEAD_DIM, B_Q_LORA, B_KV_LORA, B_ROPE)
IN_EVEN = sum(EVEN_SPLITS)

C_WINDOWS = (2, 4, 8, 16)
C_GROUPS = len(C_WINDOWS)
C_GROUP_DIM = 128
C_WIDTH = C_GROUPS * C_GROUP_DIM
D_HEADS = 4
D_MIX_ODD = C_WIDTH + D_HEADS * HEAD_DIM
ODD_SPLITS = (C_WIDTH, D_HEADS * HEAD_DIM, D_HEADS * HEAD_DIM, D_HEADS * HEAD_DIM)
IN_ODD = sum(ODD_SPLITS)

N_EVEN = (DEPTH + 1) // 2
N_ODD = DEPTH // 2

kernel_name = "hybrid_dilated_mla_pool_stickbreak_macaron"


def rms_norm(x, g):
    x32 = x.astype(jnp.float32)
    y = x32 * lax.rsqrt(jnp.mean(x32 * x32, axis=-1, keepdims=True) + NORM_EPS)
    return (y * g.astype(jnp.float32)).astype(x.dtype)


def rope(x, pos):
    d = x.shape[-1]
    inv = ROPE_THETA ** (-jnp.arange(0, d, 2, dtype=jnp.float32) / d)
    ang = pos.astype(jnp.float32)[:, None, :, None] * inv
    cos, sin = jnp.cos(ang), jnp.sin(ang)
    x32 = x.astype(jnp.float32)
    x1, x2 = x32[..., : d // 2], x32[..., d // 2:]
    return jnp.concatenate([x1 * cos - x2 * sin, x2 * cos + x1 * sin], axis=-1).astype(x.dtype)


def swiglu(x, w_gate, w_up, w_down):
    return (jax.nn.silu(x @ w_gate) * (x @ w_up)) @ w_down


def split_heads(t, n_heads):
    b, s, _ = t.shape
    return t.reshape(b, s, n_heads, -1).transpose(0, 2, 1, 3)


def merge_heads(t):
    b, h, s, d = t.shape
    return t.transpose(0, 2, 1, 3).reshape(b, s, h * d)


def dilated_attention(q, k, v):
    b, h, s, d = q.shape
    dt = q.dtype
    q32 = q.astype(jnp.float32) * (d ** -0.5)
    k32, v32 = k.astype(jnp.float32), v.astype(jnp.float32)
    qi = jnp.arange(BLOCK_Q)
    ki = jnp.arange(2 * BLOCK_Q)
    outs, lses = [], []
    for w, dil in A_PATTERNS:
        n_back = w // dil
        assert n_back <= BLOCK_Q
        length = s // dil
        nb = -(-length // BLOCK_Q)
        lp = nb * BLOCK_Q

        def to_phase(t):
            t = t.reshape(b, h, length, dil, d).transpose(0, 1, 3, 2, 4)
            t = jnp.pad(t, ((0, 0), (0, 0), (0, 0), (0, lp - length), (0, 0)))
            return t.reshape(b, h, dil, nb, BLOCK_Q, d)

        def band(t):
            prev = jnp.pad(t, ((0, 0), (0, 0), (0, 0), (1, 0), (0, 0), (0, 0)))[:, :, :, :-1]
            return jnp.concatenate([prev, t], axis=-2)

        qp = to_phase(q32)
        kb, vb = band(to_phase(k32)), band(to_phase(v32))
        logits = jnp.einsum('bhrnqd,bhrnkd->bhrnqk', qp, kb)
        dist = BLOCK_Q + qi[:, None] - ki[None, :]
        valid = ((dist >= 0) & (dist <= n_back))[None]
        first = (jnp.arange(nb) == 0)[:, None, None] & (ki < BLOCK_Q)[None, None, :]
        valid = valid & ~first
        logits = jnp.where(valid, logits, -jnp.inf)
        m = jnp.max(logits, axis=-1, keepdims=True)
        p = jnp.exp(logits - m)
        den = jnp.sum(p, axis=-1)
        o = jnp.einsum('bhrnqk,bhrnkd->bhrnqd', p, vb) / den[..., None]
        lse = m[..., 0] + jnp.log(den)
        o = o.reshape(b, h, dil, lp, d)[:, :, :, :length].transpose(0, 1, 3, 2, 4).reshape(b, h, s, d)
        lse = lse.reshape(b, h, dil, lp)[:, :, :, :length].transpose(0, 1, 3, 2).reshape(b, h, s)
        outs.append(o)
        lses.append(lse)
    wts = jax.nn.softmax(jnp.stack(lses, axis=0), axis=0)
    return jnp.sum(wts[..., None] * jnp.stack(outs, axis=0), axis=0).astype(dt)


def causal_softmax_attention(q, k, v):
    b, h, s, dk = q.shape
    dt = v.dtype
    q32 = q.astype(jnp.float32) * (dk ** -0.5)
    k32, v32 = k.astype(jnp.float32), v.astype(jnp.float32)
    qi = jnp.arange(BLOCK_Q)
    outs = []
    for bi in range(s // BLOCK_Q):
        s0 = bi * BLOCK_Q
        n_keys = s0 + BLOCK_Q
        logits = jnp.einsum('bhqd,bhkd->bhqk', q32[:, :, s0:n_keys], k32[:, :, :n_keys])
        mask = jnp.arange(n_keys)[None, :] <= (s0 + qi)[:, None]
        p = jax.nn.softmax(jnp.where(mask, logits, -jnp.inf), axis=-1)
        outs.append(jnp.einsum('bhqk,bhkd->bhqd', p, v32[:, :, :n_keys]))
    return jnp.concatenate(outs, axis=2).astype(dt)


def stick_breaking_attention(q, k, v):
    b, h, s, d = q.shape
    dt = v.dtype
    q32 = q.astype(jnp.float32) * (d ** -0.5)
    k32, v32 = k.astype(jnp.float32), v.astype(jnp.float32)
    qi = jnp.arange(BLOCK_Q)
    kk = jnp.arange(BLOCK_Q)
    strict_after = (kk[None, :] > kk[:, None]).astype(jnp.float32)
    outs = []
    for bi in range(s // BLOCK_Q):
        s0 = bi * BLOCK_Q
        n_keys = s0 + BLOCK_Q
        nk = bi + 1
        z = jnp.einsum('bhqd,bhkd->bhqk', q32[:, :, s0:n_keys], k32[:, :, :n_keys])
        mask = jnp.arange(n_keys)[None, :] < (s0 + qi)[:, None]
        ls_neg = jax.nn.log_sigmoid(-z)
        log_keep = jnp.where(mask, ls_neg, 0.0).reshape(b, h, BLOCK_Q, nk, BLOCK_Q)
        within = jnp.einsum('bhqnk,jk->bhqnj', log_keep, strict_after)
        totals = jnp.sum(log_keep, axis=-1)
        later_blocks = lax.cumsum(totals, axis=3, reverse=True) - totals
        after = (within + later_blocks[..., None]).reshape(b, h, BLOCK_Q, n_keys)
        a = jnp.where(mask, jnp.exp(z + ls_neg + after), 0.0)
        outs.append(jnp.einsum('bhqk,bhkd->bhqd', a, v32[:, :, :n_keys]))
    return jnp.concatenate(outs, axis=2).astype(dt)


def pool_mixer(u, w_group, scale):
    b, s, c = u.shape
    u32 = u.astype(jnp.float32)
    cs = jnp.concatenate([jnp.zeros((b, 1, c), jnp.float32), lax.cumsum(u32, axis=1)], axis=1)
    t = jnp.arange(s)
    outs = []
    for g, w in enumerate(C_WINDOWS):
        sl = slice(g * C_GROUP_DIM, (g + 1) * C_GROUP_DIM)
        csg = cs[..., sl]
        lo = jnp.maximum(t + 1 - w, 0)
        total = csg[:, 1:] - csg[:, lo]
        count = jnp.minimum(t + 1, w).astype(jnp.float32)
        outs.append(total / count[None, :, None] - u32[..., sl])
    pooled = jnp.stack(outs, axis=2)
    mixed = jnp.einsum('bsgc,gce->bsge', pooled, w_group.astype(jnp.float32)).reshape(b, s, c)
    return (mixed * scale.astype(jnp.float32)).astype(u.dtype)


def even_mixer(h, positions, w_in, q_norm, w_q_up, kv_norm, w_kv_up, w_out):
    b, s, _ = h.shape
    proj = h @ w_in
    offs = [int(o) for o in np.cumsum(EVEN_SPLITS)[:-1]]
    qa, ka, va, c_q, c_kv, k_rope = jnp.split(proj, offs, axis=-1)
    out_a = dilated_attention(rope(split_heads(qa, A_HEADS), positions),
                              rope(split_heads(ka, A_HEADS), positions),
                              split_heads(va, A_HEADS))
    qb = split_heads(rms_norm(c_q, q_norm) @ w_q_up, B_HEADS)
    q_nope, q_pe = qb[..., :B_NOPE], qb[..., B_NOPE:]
    kv = split_heads(rms_norm(c_kv, kv_norm) @ w_kv_up, B_HEADS)
    k_nope, v_b = kv[..., :B_NOPE], kv[..., B_NOPE:]
    k_pe = rope(k_rope[:, None], positions)
    q_full = jnp.concatenate([q_nope, rope(q_pe, positions)], axis=-1)
    k_full = jnp.concatenate([k_nope, jnp.broadcast_to(k_pe, (b, B_HEADS, s, B_ROPE))], axis=-1)
    out_b = causal_softmax_attention(q_full, k_full, v_b)
    merged = merge_heads(jnp.concatenate([out_a, out_b], axis=1))
    return merged @ w_out


def odd_mixer(h, w_in, pool_w, pool_scale, w_out):
    proj = h @ w_in
    offs = [int(o) for o in np.cumsum(ODD_SPLITS)[:-1]]
    u, qd, kd, vd = jnp.split(proj, offs, axis=-1)
    out_c = pool_mixer(u, pool_w, pool_scale)
    out_d = merge_heads(stick_breaking_attention(split_heads(qd, D_HEADS), split_heads(kd, D_HEADS),
                                                 split_heads(vd, D_HEADS)))
    return jnp.concatenate([out_c, out_d], axis=-1) @ w_out


def setup_inputs(seed: int = 0) -> dict:
    key = jax.random.key(seed)
    ks = jax.random.split(key, 20)
    f32 = jnp.float32

    def nrm(k, shape, fan_in):
        return jax.random.normal(k, shape, f32) * (fan_in ** -0.5)

    def gain(k, shape):
        return 1.0 + 0.02 * jax.random.normal(k, shape, f32)

    x = jax.random.normal(ks[0], (BATCH, SEQ, D_MODEL), f32)
    offsets = jax.random.randint(ks[1], (BATCH, 1), 0, 4096, dtype=jnp.int32)
    positions = (jnp.arange(SEQ, dtype=jnp.int32)[None, :] + offsets).astype(jnp.int32)
    return {
        "x": x,
        "positions": positions,
        "norm_g": gain(ks[2], (DEPTH, 3, D_MODEL)),
        "ffn_w_gate": nrm(ks[3], (DEPTH, 2, D_MODEL, D_FF), D_MODEL),
        "ffn_w_up": nrm(ks[4], (DEPTH, 2, D_MODEL, D_FF), D_MODEL),
        "ffn_w_down": nrm(ks[5], (DEPTH, 2, D_FF, D_MODEL), D_FF),
        "even_w_in": nrm(ks[6], (N_EVEN, D_MODEL, IN_EVEN), D_MODEL),
        "even_q_norm": gain(ks[7], (N_EVEN, B_Q_LORA)),
        "even_w_q_up": nrm(ks[8], (N_EVEN, B_Q_LORA, B_HEADS * (B_NOPE + B_ROPE)), B_Q_LORA),
        "even_kv_norm": gain(ks[9], (N_EVEN, B_KV_LORA)),
        "even_w_kv_up": nrm(ks[10], (N_EVEN, B_KV_LORA, B_HEADS * (B_NOPE + B_V)), B_KV_LORA),
        "even_w_out": nrm(ks[11], (N_EVEN, D_MIX_EVEN, D_MODEL), D_MIX_EVEN),
        "odd_w_in": nrm(ks[12], (N_ODD, D_MODEL, IN_ODD), D_MODEL),
        "odd_pool_w": nrm(ks[13], (N_ODD, C_GROUPS, C_GROUP_DIM, C_GROUP_DIM), C_GROUP_DIM),
        "odd_pool_scale": gain(ks[14], (N_ODD, C_WIDTH)),
        "odd_w_out": nrm(ks[15], (N_ODD, D_MIX_ODD, D_MODEL), D_MIX_ODD),
        "final_norm": gain(ks[16], (D_MODEL,)),
    }


def reference(x, positions, norm_g, ffn_w_gate, ffn_w_up, ffn_w_down, even_w_in, even_q_norm, even_w_q_up,
              even_kv_norm, even_w_kv_up, even_w_out, odd_w_in, odd_pool_w, odd_pool_scale, odd_w_out,
              final_norm):
    h = x
    for i in range(DEPTH):
        h = h + 0.5 * swiglu(rms_norm(h, norm_g[i, 0]), ffn_w_gate[i, 0], ffn_w_up[i, 0], ffn_w_down[i, 0])
        hn = rms_norm(h, norm_g[i, 1])
        if i % 2 == 0:
            e = i // 2
            mix = even_mixer(hn, positions, even_w_in[e], even_q_norm[e], even_w_q_up[e], even_kv_norm[e],
                             even_w_kv_up[e], even_w_out[e])
        else:
            o = i // 2
            mix = odd_mixer(hn, odd_w_in[o], odd_pool_w[o], odd_pool_scale[o], odd_w_out[o])
        h = h + mix
        h = h + 0.5 * swiglu(rms_norm(h, norm_g[i, 2]), ffn_w_gate[i, 1], ffn_w_up[i, 1], ffn_w_down[i, 1])
    return rms_norm(h, final_norm)
```

```python
import functools

import jax
import jax.numpy as jnp
import numpy as np
from jax import lax
from jax.experimental import pallas as pl
from jax.experimental.pallas import tpu as pltpu

F32 = jnp.float32
BF16 = jnp.bfloat16

HEAD_DIM = 128
LANES = 128
ROPE_THETA = 10000.0
NORM_EPS = 1e-6
A_HEADS = 6
A_DILATIONS = (1, 4, 16)
A_BACK = 128
B_HEADS = 4
B_Q_LORA = 384
B_KV_LORA = 256
B_NOPE = 64
B_ROPE = 32
B_V = 128
C_WINDOWS = (2, 4, 8, 16)
C_GROUP_DIM = 128
C_WIDTH = len(C_WINDOWS) * C_GROUP_DIM
C_HALO = 16
D_HEADS = 4
MASK_VALUE = -1e30
VMEM_LIMIT_BYTES = 56 * 1024 * 1024


def _params(*semantics):
    return pltpu.CompilerParams(dimension_semantics=semantics, vmem_limit_bytes=VMEM_LIMIT_BYTES)


def _resident(shape):
    return pl.BlockSpec(shape, lambda *_: (0,) * len(shape), pipeline_mode=pl.Buffered(1))


def _rms(x, g):
    ms = jnp.mean(x * x, axis=-1, keepdims=True)
    return x * lax.rsqrt(ms + NORM_EPS) * g


def _dot(a, b):
    return jnp.dot(a, b, preferred_element_type=F32)


def _dot_nt(a, b):
    return lax.dot_general(a, b, (((1,), (1,)), ((), ())), preferred_element_type=F32)


def _ffn_body(*refs, n_mix, final_norm):
    h_ref = refs[0]
    mix = [(refs[1 + 2 * i], refs[2 + 2 * i]) for i in range(n_mix)]
    base = 1 + 2 * n_mix
    g_ref, wg_ref, wu_ref, wd_ref = refs[base:base + 4]
    fg_ref = refs[base + 4] if final_norm else None
    o_ref = refs[-1]

    h = h_ref[...]
    for m_ref, w_ref in mix:
        h = h + _dot(m_ref[...], w_ref[...])
    xn = _rms(h, g_ref[...]).astype(BF16)
    gate = _dot(xn, wg_ref[...])
    up = _dot(xn, wu_ref[...])
    act = (gate * jax.nn.sigmoid(gate) * up).astype(BF16)
    y = h + 0.5 * _dot(act, wd_ref[...])
    if final_norm:
        y = _rms(y, fg_ref[...])
    o_ref[...] = y


def _ffn(h, mix, g, wg, wu, wd, final_g=None, *, tm):
    t, d = h.shape
    d_ff = wg.shape[1]
    row = lambda w: pl.BlockSpec((tm, w), lambda i: (i, 0))
    args, specs = [h], [row(d)]
    for m, w in mix:
        args += [m, w]
        specs += [row(m.shape[1]), _resident(w.shape)]
    args += [g, wg, wu, wd]
    specs += [_resident((1, d)), _resident((d, d_ff)), _resident((d, d_ff)), _resident((d_ff, d))]
    if final_g is not None:
        args.append(final_g)
        specs.append(_resident((1, d)))
    return pl.pallas_call(
        functools.partial(_ffn_body, n_mix=len(mix), final_norm=final_g is not None),
        grid=(t // tm,),
        in_specs=specs,
        out_specs=row(d),
        out_shape=jax.ShapeDtypeStruct((t, d), F32),
        compiler_params=_params("parallel"),
        name="ffn",
    )(*args)


def _rope_tables_body(pos_ref, inva_ref, invb_ref, ca_ref, sa_ref, cb_ref, sb_ref):
    pos = pos_ref[...].astype(F32)
    lane = lax.broadcasted_iota(jnp.int32, (1, LANES), 1)
    sign = jnp.where(lane < LANES // 2, -1.0, 1.0).astype(F32)
    ang_a = pos * inva_ref[...]
    ca_ref[...] = jnp.cos(ang_a)
    sa_ref[...] = sign * jnp.sin(ang_a)
    ang_b = pos * invb_ref[...]
    cb_ref[...] = jnp.cos(ang_b)
    sb_ref[...] = sign * jnp.sin(ang_b)


def _rope_tables(pos, inv_a, inv_b, *, tm):
    t = pos.shape[0]
    tab = pl.BlockSpec((tm, LANES), lambda i: (i, 0))
    return pl.pallas_call(
        _rope_tables_body,
        grid=(t // tm,),
        in_specs=[pl.BlockSpec((tm, 1), lambda i: (i, 0)), _resident((1, LANES)), _resident((1, LANES))],
        out_specs=[tab] * 4,
        out_shape=[jax.ShapeDtypeStruct((t, LANES), F32)] * 4,
        compiler_params=_params("parallel"),
        name="rope_tables",
    )(pos, inv_a, inv_b)


def _rotate(x, c, s):
    return x * c + pltpu.roll(x, LANES // 2, 1) * s


def _even_proj_body(h_ref, g_ref, win_ref, ca_ref, sa_ref, cb_ref, sb_ref, qn_ref, wq_ref, kvn_ref, wkv_ref,
                    qa_ref, ka_ref, va_ref, qb_ref, kb_ref, vb_ref):
    xn = _rms(h_ref[...], g_ref[...]).astype(BF16)
    proj = _dot(xn, win_ref[...])
    ca, sa, cb, sb = ca_ref[...], sa_ref[...], cb_ref[...], sb_ref[...]
    a_w = A_HEADS * HEAD_DIM
    blk = lambda base, i: slice(base + i * LANES, base + (i + 1) * LANES)
    for hd in range(A_HEADS):
        qa_ref[hd] = (_rotate(proj[:, blk(0, hd)], ca, sa) * (HEAD_DIM ** -0.5)).astype(BF16)
        ka_ref[hd] = _rotate(proj[:, blk(a_w, hd)], ca, sa).astype(BF16)
        va_ref[hd] = proj[:, blk(2 * a_w, hd)].astype(BF16)
    o_cq = 3 * a_w
    o_ckv = o_cq + B_Q_LORA
    o_kr = o_ckv + B_KV_LORA
    cq = _rms(proj[:, o_cq:o_ckv], qn_ref[...]).astype(BF16)
    ckv = _rms(proj[:, o_ckv:o_kr], kvn_ref[...]).astype(BF16)
    qb = _dot(cq, wq_ref[...])
    kv = _dot(ckv, wkv_ref[...])
    k_pe = _rotate(proj[:, o_kr:o_kr + LANES], cb, sb)
    q_scale = (B_NOPE + B_ROPE) ** -0.5
    for hd in range(B_HEADS):
        qb_ref[hd] = (_rotate(qb[:, blk(0, hd)], cb, sb) * q_scale).astype(BF16)
        kb_ref[hd] = (kv[:, blk(0, hd)] + k_pe).astype(BF16)
        vb_ref[hd] = kv[:, blk(B_HEADS * LANES, hd)].astype(BF16)


def _even_proj(h, g, w_in, tabs, q_norm, w_q, kv_norm, w_kv, *, tm):
    t, d = h.shape
    row = lambda w: pl.BlockSpec((tm, w), lambda i: (i, 0))
    heads = lambda n: pl.BlockSpec((n, tm, LANES), lambda i: (0, i, 0))
    shp = lambda n: jax.ShapeDtypeStruct((n, t, LANES), BF16)
    return pl.pallas_call(
        _even_proj_body,
        grid=(t // tm,),
        in_specs=[row(d), _resident((1, d)), _resident(w_in.shape)] + [row(LANES)] * 4
        + [_resident(q_norm.shape), _resident(w_q.shape), _resident(kv_norm.shape), _resident(w_kv.shape)],
        out_specs=[heads(A_HEADS)] * 3 + [heads(B_HEADS)] * 3,
        out_shape=[shp(A_HEADS)] * 3 + [shp(B_HEADS)] * 3,
        compiler_params=_params("parallel"),
        name="even_proj",
    )(h, g, w_in, *tabs, q_norm, w_q, kv_norm, w_kv)


A_TILE = A_BACK * max(A_DILATIONS)


def _dilated_body(q_ref, k_ref, v_ref, o_ref, qf, kf, vf, o_s, lse_s):
    i = pl.program_id(2)

    @pl.when(i > 0)
    def _():
        kf[0:A_TILE] = kf[A_TILE:2 * A_TILE]
        vf[0:A_TILE] = vf[A_TILE:2 * A_TILE]

    @pl.when(i == 0)
    def _():
        kf[0:A_TILE] = jnp.zeros((A_TILE, HEAD_DIM), F32)
        vf[0:A_TILE] = jnp.zeros((A_TILE, HEAD_DIM), F32)

    kf[A_TILE:2 * A_TILE] = k_ref[0].astype(F32)
    vf[A_TILE:2 * A_TILE] = v_ref[0].astype(F32)
    qf[...] = q_ref[0].astype(F32)

    qi = lax.broadcasted_iota(jnp.int32, (A_BACK, 2 * A_BACK), 0)
    ki = lax.broadcasted_iota(jnp.int32, (A_BACK, 2 * A_BACK), 1)
    band = (ki >= qi) & (ki <= qi + A_BACK)
    band_first = band & (ki >= jnp.where(i > 0, 0, A_BACK))

    for p, dil in enumerate(A_DILATIONS):
        for r in range(dil):
            for j in range(max(A_DILATIONS) // dil):
                q_rows = pl.ds(r + j * A_BACK * dil, A_BACK, stride=dil)
                kv_rows = pl.ds(A_TILE + (j - 1) * A_BACK * dil + r, 2 * A_BACK, stride=dil)
                qb = qf[q_rows, :].astype(BF16)
                kb = kf[kv_rows, :].astype(BF16)
                vb = vf[kv_rows, :].astype(BF16)
                s = jnp.where(band_first if j == 0 else band, _dot_nt(qb, kb), MASK_VALUE)
                m = jnp.max(s, axis=-1, keepdims=True)
                e = jnp.exp(s - m)
                den = jnp.sum(e, axis=-1, keepdims=True)
                o_s[p, q_rows, :] = _dot(e.astype(BF16), vb) / den
                lse_s[p, q_rows, :] = jnp.broadcast_to(m + jnp.log(den), (A_BACK, HEAD_DIM))

    n_pat = len(A_DILATIONS)
    lse = [lse_s[p] for p in range(n_pat)]
    top = functools.reduce(jnp.maximum, lse)
    wts = [jnp.exp(x - top) for x in lse]
    total = functools.reduce(jnp.add, wts)
    out = functools.reduce(jnp.add, [w * o_s[p] for p, w in enumerate(wts)]) / total
    o_ref[...] = out.astype(BF16)


def _dilated_attn(q, k, v, *, batch, seq):
    n_tiles = seq // A_TILE
    heads, t, _ = q.shape
    blk = pl.BlockSpec((1, A_TILE, HEAD_DIM), lambda b, h, i: (h, b * n_tiles + i, 0))
    return pl.pallas_call(
        _dilated_body,
        grid=(batch, heads, n_tiles),
        in_specs=[blk, blk, blk],
        out_specs=pl.BlockSpec((A_TILE, HEAD_DIM), lambda b, h, i: (b * n_tiles + i, h)),
        out_shape=jax.ShapeDtypeStruct((t, heads * HEAD_DIM), BF16),
        scratch_shapes=[
            pltpu.VMEM((A_TILE, HEAD_DIM), F32),
            pltpu.VMEM((2 * A_TILE, HEAD_DIM), F32),
            pltpu.VMEM((2 * A_TILE, HEAD_DIM), F32),
            pltpu.VMEM((len(A_DILATIONS), A_TILE, HEAD_DIM), F32),
            pltpu.VMEM((len(A_DILATIONS), A_TILE, HEAD_DIM), F32),
        ],
        compiler_params=_params("parallel", "parallel", "arbitrary"),
        name="dilated_attn",
    )(q, k, v)


def _mla_body(q_ref, k_ref, v_ref, o_ref, m_s, l_s, acc_s, *, tq):
    qi = pl.program_id(2)
    q = q_ref[0]
    m_s[...] = jnp.full(m_s.shape, MASK_VALUE, F32)
    l_s[...] = jnp.zeros(l_s.shape, F32)
    acc_s[...] = jnp.zeros(acc_s.shape, F32)

    def step(kb, diagonal):
        rows = pl.ds(pl.multiple_of(kb * tq, tq), tq)
        s = _dot_nt(q, k_ref[0, rows, :])
        if diagonal:
            row = lax.broadcasted_iota(jnp.int32, (tq, tq), 0)
            col = lax.broadcasted_iota(jnp.int32, (tq, tq), 1)
            s = jnp.where(col <= row, s, MASK_VALUE)
        m_prev = m_s[...]
        m_new = jnp.maximum(m_prev, jnp.max(s, axis=-1, keepdims=True))
        alpha = jnp.exp(m_prev - m_new)
        p = jnp.exp(s - m_new)
        l_s[...] = alpha * l_s[...] + jnp.sum(p, axis=-1, keepdims=True)
        acc_s[...] = alpha * acc_s[...] + _dot(p.astype(BF16), v_ref[0, rows, :])
        m_s[...] = m_new

    def full_block(kb, carry):
        step(kb, False)
        return carry

    lax.fori_loop(0, qi, full_block, 0)
    step(qi, True)
    o_ref[...] = (acc_s[...] / l_s[...]).astype(BF16)


def _mla_attn(q, k, v, *, batch, seq, tq):
    heads, t, _ = q.shape
    nq = seq // tq
    kv = pl.BlockSpec((1, seq, LANES), lambda h, b, i: (h, b, 0))
    return pl.pallas_call(
        functools.partial(_mla_body, tq=tq),
        grid=(heads, batch, nq),
        in_specs=[pl.BlockSpec((1, tq, LANES), lambda h, b, i: (h, b * nq + i, 0)), kv, kv],
        out_specs=pl.BlockSpec((tq, LANES), lambda h, b, i: (b * nq + i, h)),
        out_shape=jax.ShapeDtypeStruct((t, heads * LANES), BF16),
        scratch_shapes=[pltpu.VMEM((tq, 1), F32), pltpu.VMEM((tq, 1), F32), pltpu.VMEM((tq, LANES), F32)],
        compiler_params=_params("parallel", "parallel", "arbitrary"),
        name="mla_attn",
    )(q, k, v)


def _odd_proj_body(h_ref, g_ref, win_ref, pw_ref, ps_ref, oc_ref, qd_ref, kd_ref, vd_ref, ext, *, tm, tiles_per_seq):
    i = pl.program_id(0)
    xn = _rms(h_ref[...], g_ref[...]).astype(BF16)
    proj = _dot(xn, win_ref[...])

    @pl.when(i % tiles_per_seq == 0)
    def _():
        ext[0:C_HALO, :] = jnp.zeros((C_HALO, C_WIDTH), F32)

    ext[C_HALO:C_HALO + tm, :] = proj[:, 0:C_WIDTH]
    t_in_seq = (i % tiles_per_seq) * tm + lax.broadcasted_iota(jnp.int32, (tm, 1), 0)
    for g, w in enumerate(C_WINDOWS):
        cols = slice(g * C_GROUP_DIM, (g + 1) * C_GROUP_DIM)
        s = ext[C_HALO - (w - 1):C_HALO + tm, cols]
        step = 1
        while step < w:
            s = s[step:] + s[:-step]
            step *= 2
        count = jnp.minimum(t_in_seq + 1, w).astype(F32)
        pooled = s / count - proj[:, cols]
        oc_ref[:, cols] = (_dot(pooled.astype(BF16), pw_ref[g]) * ps_ref[:, cols]).astype(BF16)
    ext[0:C_HALO, :] = ext[tm:tm + C_HALO, :]

    dw = D_HEADS * HEAD_DIM
    for hd in range(D_HEADS):
        lanes = lambda base: slice(base + hd * LANES, base + (hd + 1) * LANES)
        qd_ref[hd] = (proj[:, lanes(C_WIDTH)] * (HEAD_DIM ** -0.5)).astype(BF16)
        kd_ref[hd] = proj[:, lanes(C_WIDTH + dw)].astype(BF16)
        vd_ref[hd] = proj[:, lanes(C_WIDTH + 2 * dw)].astype(BF16)


def _odd_proj(h, g, w_in, pool_w, pool_scale, *, seq, tm):
    t, d = h.shape
    row = lambda w: pl.BlockSpec((tm, w), lambda i: (i, 0))
    heads = pl.BlockSpec((D_HEADS, tm, LANES), lambda i: (0, i, 0))
    shp = jax.ShapeDtypeStruct((D_HEADS, t, LANES), BF16)
    return pl.pallas_call(
        functools.partial(_odd_proj_body, tm=tm, tiles_per_seq=seq // tm),
        grid=(t // tm,),
        in_specs=[row(d), _resident((1, d)), _resident(w_in.shape), _resident(pool_w.shape), _resident(pool_scale.shape)],
        out_specs=[row(C_WIDTH), heads, heads, heads],
        out_shape=[jax.ShapeDtypeStruct((t, C_WIDTH), BF16), shp, shp, shp],
        scratch_shapes=[pltpu.VMEM((C_HALO + tm, C_WIDTH), F32)],
        compiler_params=_params("arbitrary"),
        name="odd_proj",
    )(h, g, w_in, pool_w, pool_scale)


def _sb_body(q_ref, k_ref, v_ref, o_ref, carry_s, acc_s, *, tq, tk):
    qi = pl.program_id(2)
    ratio = tq // tk
    q = q_ref[0]
    carry_s[...] = jnp.zeros(carry_s.shape, F32)
    acc_s[...] = jnp.zeros(acc_s.shape, F32)
    later = (lax.broadcasted_iota(jnp.int32, (tk, tk), 0) > lax.broadcasted_iota(jnp.int32, (tk, tk), 1))
    later = jnp.where(later, 1.0, 0.0).astype(BF16)

    def step(kb, diag_block):
        rows = pl.ds(pl.multiple_of(kb * tk, tk), tk)
        z = _dot_nt(q, k_ref[0, rows, :])
        softplus = jnp.maximum(z, 0.0) + jnp.log(1.0 + jnp.exp(-jnp.abs(z)))
        arg = z - softplus
        if diag_block is None:
            keep = softplus
        else:
            row = lax.broadcasted_iota(jnp.int32, (tq, tk), 0)
            col = lax.broadcasted_iota(jnp.int32, (tq, tk), 1) + diag_block * tk
            mask = col < row
            keep = jnp.where(mask, softplus, 0.0)
        carry = carry_s[...]
        a = jnp.exp(arg - _dot(keep.astype(BF16), later) - carry)
        if diag_block is not None:
            a = jnp.where(mask, a, 0.0)
        acc_s[...] += _dot(a.astype(BF16), v_ref[0, rows, :])
        carry_s[...] = carry + jnp.sum(keep, axis=-1, keepdims=True)

    for c in reversed(range(ratio)):
        step(qi * ratio + c, c)

    def full_block(n, carry):
        step(qi * ratio - 1 - n, None)
        return carry

    lax.fori_loop(0, qi * ratio, full_block, 0)
    o_ref[...] = acc_s[...].astype(BF16)


def _sb_attn(q, k, v, *, batch, seq, tq, tk):
    heads, t, _ = q.shape
    nq = seq // tq
    kv = pl.BlockSpec((1, seq, LANES), lambda h, b, i: (h, b, 0))
    return pl.pallas_call(
        functools.partial(_sb_body, tq=tq, tk=tk),
        grid=(heads, batch, nq),
        in_specs=[pl.BlockSpec((1, tq, LANES), lambda h, b, i: (h, b * nq + i, 0)), kv, kv],
        out_specs=pl.BlockSpec((tq, LANES), lambda h, b, i: (b * nq + i, h)),
        out_shape=jax.ShapeDtypeStruct((t, heads * LANES), BF16),
        scratch_shapes=[pltpu.VMEM((tq, 1), F32), pltpu.VMEM((tq, LANES), F32)],
        compiler_params=_params("parallel", "parallel", "arbitrary"),
        name="sb_attn",
    )(q, k, v)


def _latent_head_lanes():
    half_pe, half_nope = B_ROPE // 2, B_NOPE // 2
    pe = np.concatenate([np.arange(half_pe), LANES // 2 + np.arange(half_pe)])
    nope = np.concatenate([half_pe + np.arange(half_nope), LANES // 2 + half_pe + np.arange(half_nope)])
    return pe, nope


def _scatter_cols(w, src_cols, dst_cols, width):
    out = jnp.zeros((w.shape[0], width), w.dtype)
    return out.at[:, np.asarray(dst_cols)].set(w[:, np.asarray(src_cols)])


def _even_weights(w_in, w_q_up, w_kv_up):
    pe, nope = _latent_head_lanes()
    a_w = A_HEADS * HEAD_DIM
    o_kr = 3 * a_w + B_Q_LORA + B_KV_LORA
    k_rope = _scatter_cols(w_in, o_kr + np.arange(B_ROPE), pe, LANES)
    w_in_p = jnp.concatenate([w_in[:, :o_kr], k_rope], axis=1).astype(BF16)
    qd, kvd = B_NOPE + B_ROPE, B_NOPE + B_V
    src_q, dst_q, src_k, dst_k, src_v = [], [], [], [], []
    for hd in range(B_HEADS):
        src_q += list(hd * qd + np.arange(B_NOPE)) + list(hd * qd + B_NOPE + np.arange(B_ROPE))
        dst_q += list(hd * LANES + nope) + list(hd * LANES + pe)
        src_k += list(hd * kvd + np.arange(B_NOPE))
        dst_k += list(hd * LANES + nope)
        src_v += list(hd * kvd + B_NOPE + np.arange(B_V))
    w_q = _scatter_cols(w_q_up, src_q, dst_q, B_HEADS * LANES).astype(BF16)
    w_k = _scatter_cols(w_kv_up, src_k, dst_k, B_HEADS * LANES)
    w_kv = jnp.concatenate([w_k, w_kv_up[:, np.asarray(src_v)]], axis=1).astype(BF16)
    return w_in_p, w_q, w_kv


def _rope_frequencies():
    inv_a = ROPE_THETA ** (-jnp.arange(0, HEAD_DIM, 2, dtype=F32) / HEAD_DIM)
    inv_b = ROPE_THETA ** (-jnp.arange(0, B_ROPE, 2, dtype=F32) / B_ROPE)
    pe, _ = _latent_head_lanes()
    inv_b_lanes = jnp.zeros((LANES,), F32).at[pe].set(jnp.concatenate([inv_b, inv_b]))
    return jnp.concatenate([inv_a, inv_a])[None, :], inv_b_lanes[None, :]


def kernel(x, positions, norm_g, ffn_w_gate, ffn_w_up, ffn_w_down, even_w_in, even_q_norm, even_w_q_up, even_kv_norm, even_w_kv_up, even_w_out, odd_w_in, odd_pool_w, odd_pool_scale, odd_w_out, final_norm):
    batch, seq, d = x.shape
    t = batch * seq
    assert seq % A_TILE == 0
    tm = 256
    h = x.reshape(t, d).astype(F32)
    pos = positions.reshape(t, 1).astype(jnp.int32)
    bf = lambda w: w.astype(BF16)
    gain = lambda i, j: norm_g[i, j][None, :].astype(F32)
    ffn_w = lambda i, j: (bf(ffn_w_gate[i, j]), bf(ffn_w_up[i, j]), bf(ffn_w_down[i, j]))

    h = _ffn(h, [], gain(0, 0), *ffn_w(0, 0), tm=tm)
    tabs = _rope_tables(pos, *_rope_frequencies(), tm=1024)
    w_in_p, w_q, w_kv = _even_weights(even_w_in[0], even_w_q_up[0], even_w_kv_up[0])
    qa, ka, va, qb, kb, vb = _even_proj(h, gain(0, 1), w_in_p, tabs, even_q_norm[0][None, :].astype(F32), w_q,
                                        even_kv_norm[0][None, :].astype(F32), w_kv, tm=tm)
    out_a = _dilated_attn(qa, ka, va, batch=batch, seq=seq)
    out_b = _mla_attn(qb, kb, vb, batch=batch, seq=seq, tq=512)
    a_w = A_HEADS * HEAD_DIM
    w_out = bf(even_w_out[0])
    h = _ffn(h, [(out_a, w_out[:a_w]), (out_b, w_out[a_w:])], gain(0, 2), *ffn_w(0, 1), tm=tm)

    h = _ffn(h, [], gain(1, 0), *ffn_w(1, 0), tm=tm)
    out_c, qd, kd, vd = _odd_proj(h, gain(1, 1), bf(odd_w_in[0]), bf(odd_pool_w[0]),
                                  odd_pool_scale[0][None, :].astype(F32), seq=seq, tm=tm)
    out_d = _sb_attn(qd, kd, vd, batch=batch, seq=seq, tq=512, tk=256)
    w_out = bf(odd_w_out[0])
    h = _ffn(h, [(out_c, w_out[:C_WIDTH]), (out_d, w_out[C_WIDTH:])], gain(1, 2), *ffn_w(1, 1),
             final_norm[None, :].astype(F32), tm=tm)
    return h.reshape(batch, seq, d).astype(x.dtype)
```

```python
import functools

import jax
import jax.numpy as jnp
import numpy as np
from jax import lax
from jax.experimental import pallas as pl
from jax.experimental.pallas import tpu as pltpu

F32 = jnp.float32
BF16 = jnp.bfloat16

HEAD_DIM = 128
LANES = 128
ROPE_THETA = 10000.0
NORM_EPS = 1e-6
A_HEADS = 6
A_DILATIONS = (1, 4, 16)
A_BACK = 128
B_HEADS = 4
B_Q_LORA = 384
B_KV_LORA = 256
B_NOPE = 64
B_ROPE = 32
B_V = 128
C_WINDOWS = (2, 4, 8, 16)
C_GROUP_DIM = 128
C_WIDTH = len(C_WINDOWS) * C_GROUP_DIM
C_HALO = 16
D_HEADS = 4
MASK_VALUE = -1e30
LOG2_E = 1.4426950408889634
VMEM_LIMIT_BYTES = 56 * 1024 * 1024


def _params(*semantics):
    return pltpu.CompilerParams(dimension_semantics=semantics, vmem_limit_bytes=VMEM_LIMIT_BYTES)


def _resident(shape):
    return pl.BlockSpec(shape, lambda *_: (0,) * len(shape), pipeline_mode=pl.Buffered(1))


def _rms(x, g):
    ms = jnp.mean(x * x, axis=-1, keepdims=True)
    return x * lax.rsqrt(ms + NORM_EPS) * g


def _dot(a, b):
    return jnp.dot(a, b, preferred_element_type=F32)


def _dot_nt(a, b):
    return lax.dot_general(a, b, (((1,), (1,)), ((), ())), preferred_element_type=F32)


def _ffn_body(*refs, n_mix, final_norm):
    h_ref = refs[0]
    mix = [(refs[1 + 2 * i], refs[2 + 2 * i]) for i in range(n_mix)]
    base = 1 + 2 * n_mix
    g_ref, wg_ref, wu_ref, wd_ref = refs[base:base + 4]
    fg_ref = refs[base + 4] if final_norm else None
    o_ref = refs[-1]

    h = h_ref[...]
    for m_ref, w_ref in mix:
        h = h + _dot(m_ref[...], w_ref[...])
    xn = _rms(h, g_ref[...]).astype(BF16)
    gate = _dot(xn, wg_ref[...])
    up = _dot(xn, wu_ref[...])
    act = (gate * jax.nn.sigmoid(gate) * up).astype(BF16)
    y = h + 0.5 * _dot(act, wd_ref[...])
    if final_norm:
        y = _rms(y, fg_ref[...])
    o_ref[...] = y


def _ffn(h, mix, g, wg, wu, wd, final_g=None, *, tm):
    t, d = h.shape
    d_ff = wg.shape[1]
    row = lambda w: pl.BlockSpec((tm, w), lambda i: (i, 0))
    args, specs = [h], [row(d)]
    for m, w in mix:
        args += [m, w]
        specs += [row(m.shape[1]), _resident(w.shape)]
    args += [g, wg, wu, wd]
    specs += [_resident((1, d)), _resident((d, d_ff)), _resident((d, d_ff)), _resident((d_ff, d))]
    if final_g is not None:
        args.append(final_g)
        specs.append(_resident((1, d)))
    return pl.pallas_call(
        functools.partial(_ffn_body, n_mix=len(mix), final_norm=final_g is not None),
        grid=(t // tm,),
        in_specs=specs,
        out_specs=row(d),
        out_shape=jax.ShapeDtypeStruct((t, d), F32),
        compiler_params=_params("parallel"),
        name="ffn",
    )(*args)


def _rope_tables_body(pos_ref, inva_ref, invb_ref, ca_ref, sa_ref, cb_ref, sb_ref):
    pos = pos_ref[...].astype(F32)
    lane = lax.broadcasted_iota(jnp.int32, (1, LANES), 1)
    sign = jnp.where(lane < LANES // 2, -1.0, 1.0).astype(F32)
    ang_a = pos * inva_ref[...]
    ca_ref[...] = jnp.cos(ang_a)
    sa_ref[...] = sign * jnp.sin(ang_a)
    ang_b = pos * invb_ref[...]
    cb_ref[...] = jnp.cos(ang_b)
    sb_ref[...] = sign * jnp.sin(ang_b)


def _rope_tables(pos, inv_a, inv_b, *, tm):
    t = pos.shape[0]
    tab = pl.BlockSpec((tm, LANES), lambda i: (i, 0))
    return pl.pallas_call(
        _rope_tables_body,
        grid=(t // tm,),
        in_specs=[pl.BlockSpec((tm, 1), lambda i: (i, 0)), _resident((1, LANES)), _resident((1, LANES))],
        out_specs=[tab] * 4,
        out_shape=[jax.ShapeDtypeStruct((t, LANES), F32)] * 4,
        compiler_params=_params("parallel"),
        name="rope_tables",
    )(pos, inv_a, inv_b)


def _rotate(x, c, s):
    return x * c + pltpu.roll(x, LANES // 2, 1) * s


def _even_proj_body(h_ref, g_ref, win_ref, ca_ref, sa_ref, cb_ref, sb_ref, qn_ref, wq_ref, kvn_ref, wkv_ref,
                    qa_ref, ka_ref, va_ref, qb_ref, kb_ref, vb_ref):
    xn = _rms(h_ref[...], g_ref[...]).astype(BF16)
    proj = _dot(xn, win_ref[...])
    ca, sa, cb, sb = ca_ref[...], sa_ref[...], cb_ref[...], sb_ref[...]
    a_w = A_HEADS * HEAD_DIM
    blk = lambda base, i: slice(base + i * LANES, base + (i + 1) * LANES)
    for hd in range(A_HEADS):
        qa_ref[hd] = (_rotate(proj[:, blk(0, hd)], ca, sa) * (HEAD_DIM ** -0.5)).astype(BF16)
        ka_ref[hd] = _rotate(proj[:, blk(a_w, hd)], ca, sa).astype(BF16)
        va_ref[hd] = proj[:, blk(2 * a_w, hd)].astype(BF16)
    o_cq = 3 * a_w
    o_ckv = o_cq + B_Q_LORA
    o_kr = o_ckv + B_KV_LORA
    cq = _rms(proj[:, o_cq:o_ckv], qn_ref[...]).astype(BF16)
    ckv = _rms(proj[:, o_ckv:o_kr], kvn_ref[...]).astype(BF16)
    qb = _dot(cq, wq_ref[...])
    kv = _dot(ckv, wkv_ref[...])
    k_pe = _rotate(proj[:, o_kr:o_kr + LANES], cb, sb)
    q_scale = (B_NOPE + B_ROPE) ** -0.5 * LOG2_E
    for hd in range(B_HEADS):
        qb_ref[hd] = (_rotate(qb[:, blk(0, hd)], cb, sb) * q_scale).astype(BF16)
        kb_ref[hd] = (kv[:, blk(0, hd)] + k_pe).astype(BF16)
        vb_ref[hd] = kv[:, blk(B_HEADS * LANES, hd)].astype(BF16)


def _even_proj(h, g, w_in, tabs, q_norm, w_q, kv_norm, w_kv, *, tm):
    t, d = h.shape
    row = lambda w: pl.BlockSpec((tm, w), lambda i: (i, 0))
    heads = lambda n: pl.BlockSpec((n, tm, LANES), lambda i: (0, i, 0))
    shp = lambda n: jax.ShapeDtypeStruct((n, t, LANES), BF16)
    return pl.pallas_call(
        _even_proj_body,
        grid=(t // tm,),
        in_specs=[row(d), _resident((1, d)), _resident(w_in.shape)] + [row(LANES)] * 4
        + [_resident(q_norm.shape), _resident(w_q.shape), _resident(kv_norm.shape), _resident(w_kv.shape)],
        out_specs=[heads(A_HEADS)] * 3 + [heads(B_HEADS)] * 3,
        out_shape=[shp(A_HEADS)] * 3 + [shp(B_HEADS)] * 3,
        compiler_params=_params("parallel"),
        name="even_proj",
    )(h, g, w_in, *tabs, q_norm, w_q, kv_norm, w_kv)


A_TILE = A_BACK * max(A_DILATIONS)


def _dilated_body(q_ref, k_ref, v_ref, o_ref, qf, kf, vf, o_s, lse_s):
    i = pl.program_id(2)

    @pl.when(i > 0)
    def _():
        kf[0:A_TILE] = kf[A_TILE:2 * A_TILE]
        vf[0:A_TILE] = vf[A_TILE:2 * A_TILE]

    @pl.when(i == 0)
    def _():
        kf[0:A_TILE] = jnp.zeros((A_TILE, HEAD_DIM), F32)
        vf[0:A_TILE] = jnp.zeros((A_TILE, HEAD_DIM), F32)

    kf[A_TILE:2 * A_TILE] = k_ref[0].astype(F32)
    vf[A_TILE:2 * A_TILE] = v_ref[0].astype(F32)
    qf[...] = q_ref[0].astype(F32)

    qi = lax.broadcasted_iota(jnp.int32, (A_BACK, 2 * A_BACK), 0)
    ki = lax.broadcasted_iota(jnp.int32, (A_BACK, 2 * A_BACK), 1)
    band = (ki >= qi) & (ki <= qi + A_BACK)
    band_first = band & (ki >= jnp.where(i > 0, 0, A_BACK))

    for p, dil in enumerate(A_DILATIONS):
        for r in range(dil):
            for j in range(max(A_DILATIONS) // dil):
                q_rows = pl.ds(r + j * A_BACK * dil, A_BACK, stride=dil)
                kv_rows = pl.ds(A_TILE + (j - 1) * A_BACK * dil + r, 2 * A_BACK, stride=dil)
                qb = qf[q_rows, :].astype(BF16)
                kb = kf[kv_rows, :].astype(BF16)
                vb = vf[kv_rows, :].astype(BF16)
                s = jnp.where(band_first if j == 0 else band, _dot_nt(qb, kb), MASK_VALUE)
                m = jnp.max(s, axis=-1, keepdims=True)
                e = jnp.exp(s - m)
                den = jnp.sum(e, axis=-1, keepdims=True)
                o_s[p, q_rows, :] = _dot(e.astype(BF16), vb) / den
                lse_s[p, q_rows, :] = jnp.broadcast_to(m + jnp.log(den), (A_BACK, HEAD_DIM))

    n_pat = len(A_DILATIONS)
    lse = [lse_s[p] for p in range(n_pat)]
    top = functools.reduce(jnp.maximum, lse)
    wts = [jnp.exp(x - top) for x in lse]
    total = functools.reduce(jnp.add, wts)
    out = functools.reduce(jnp.add, [w * o_s[p] for p, w in enumerate(wts)]) / total
    o_ref[...] = out.astype(BF16)


def _dilated_attn(q, k, v, *, batch, seq):
    n_tiles = seq // A_TILE
    heads, t, _ = q.shape
    blk = pl.BlockSpec((1, A_TILE, HEAD_DIM), lambda b, h, i: (h, b * n_tiles + i, 0))
    return pl.pallas_call(
        _dilated_body,
        grid=(batch, heads, n_tiles),
        in_specs=[blk, blk, blk],
        out_specs=pl.BlockSpec((A_TILE, HEAD_DIM), lambda b, h, i: (b * n_tiles + i, h)),
        out_shape=jax.ShapeDtypeStruct((t, heads * HEAD_DIM), BF16),
        scratch_shapes=[
            pltpu.VMEM((A_TILE, HEAD_DIM), F32),
            pltpu.VMEM((2 * A_TILE, HEAD_DIM), F32),
            pltpu.VMEM((2 * A_TILE, HEAD_DIM), F32),
            pltpu.VMEM((len(A_DILATIONS), A_TILE, HEAD_DIM), F32),
            pltpu.VMEM((len(A_DILATIONS), A_TILE, HEAD_DIM), F32),
        ],
        compiler_params=_params("parallel", "parallel", "arbitrary"),
        name="dilated_attn",
    )(q, k, v)


def _mla_body(q_ref, k_ref, v_ref, o_ref, s_a, s_b, m_s, l_s, acc_s, *, tq, tk):
    qi = pl.program_id(2)
    ratio = tq // tk
    n_tiles = tk // LANES
    q = q_ref[0]
    m_s[...] = jnp.full(m_s.shape, MASK_VALUE, F32)
    l_s[...] = jnp.zeros(l_s.shape, F32)
    acc_s[...] = jnp.zeros(acc_s.shape, F32)

    def key_rows(kb):
        return pl.ds(pl.multiple_of(kb * tk, tk), tk)

    def logits(kb, s_ref):
        s_ref[...] = _dot_nt(q, k_ref[0, key_rows(kb), :])

    def softmax_av(s_ref, kb, diag_block):
        tiles = [s_ref[:, c * LANES:(c + 1) * LANES] for c in range(n_tiles)]
        if diag_block is not None:
            row = lax.broadcasted_iota(jnp.int32, (tq, LANES), 0)
            col = lax.broadcasted_iota(jnp.int32, (tq, LANES), 1) + diag_block * tk
            tiles = [jnp.where(col + c * LANES <= row, x, MASK_VALUE) for c, x in enumerate(tiles)]
        m_prev = m_s[...]
        m_new = jnp.maximum(m_prev, jnp.max(functools.reduce(jnp.maximum, tiles), axis=-1, keepdims=True))
        alpha = jnp.exp2(m_prev - m_new)
        p = [jnp.exp2(x - m_new) for x in tiles]
        l_s[...] = alpha * l_s[...] + functools.reduce(jnp.add, p)
        pb = jnp.concatenate([x.astype(BF16) for x in p], axis=1)
        acc_s[...] = alpha * acc_s[...] + _dot(pb, v_ref[0, key_rows(kb), :])
        m_s[...] = m_new

    first_diag = qi * ratio
    logits(0, s_a)

    def pair(u, carry):
        logits(2 * u + 1, s_b)
        softmax_av(s_a, 2 * u, None)
        logits(2 * u + 2, s_a)
        softmax_av(s_b, 2 * u + 1, None)
        return carry

    lax.fori_loop(0, first_diag // 2, pair, 0)
    logits(first_diag + 1, s_b)
    softmax_av(s_a, first_diag, 0)
    softmax_av(s_b, first_diag + 1, 1)
    o_ref[...] = (acc_s[...] / jnp.sum(l_s[...], axis=-1, keepdims=True)).astype(BF16)


def _mla_attn(q, k, v, *, batch, seq, tq, tk):
    assert tq == 2 * tk
    heads, t, _ = q.shape
    nq = seq // tq
    kv = pl.BlockSpec((1, seq, LANES), lambda h, b, i: (h, b, 0))
    return pl.pallas_call(
        functools.partial(_mla_body, tq=tq, tk=tk),
        grid=(heads, batch, nq),
        in_specs=[pl.BlockSpec((1, tq, LANES), lambda h, b, i: (h, b * nq + i, 0)), kv, kv],
        out_specs=pl.BlockSpec((tq, LANES), lambda h, b, i: (b * nq + i, h)),
        out_shape=jax.ShapeDtypeStruct((t, heads * LANES), BF16),
        scratch_shapes=[pltpu.VMEM((tq, tk), F32), pltpu.VMEM((tq, tk), F32), pltpu.VMEM((tq, LANES), F32),
                        pltpu.VMEM((tq, LANES), F32), pltpu.VMEM((tq, LANES), F32)],
        compiler_params=_params("parallel", "parallel", "arbitrary"),
        name="mla_attn",
    )(q, k, v)


def _odd_proj_body(h_ref, g_ref, win_ref, pw_ref, ps_ref, oc_ref, qd_ref, kd_ref, vd_ref, ext, *, tm, tiles_per_seq):
    i = pl.program_id(0)
    xn = _rms(h_ref[...], g_ref[...]).astype(BF16)
    proj = _dot(xn, win_ref[...])

    @pl.when(i % tiles_per_seq == 0)
    def _():
        ext[0:C_HALO, :] = jnp.zeros((C_HALO, C_WIDTH), F32)

    ext[C_HALO:C_HALO + tm, :] = proj[:, 0:C_WIDTH]
    t_in_seq = (i % tiles_per_seq) * tm + lax.broadcasted_iota(jnp.int32, (tm, 1), 0)
    for g, w in enumerate(C_WINDOWS):
        cols = slice(g * C_GROUP_DIM, (g + 1) * C_GROUP_DIM)
        s = ext[C_HALO - (w - 1):C_HALO + tm, cols]
        step = 1
        while step < w:
            s = s[step:] + s[:-step]
            step *= 2
        count = jnp.minimum(t_in_seq + 1, w).astype(F32)
        pooled = s / count - proj[:, cols]
        oc_ref[:, cols] = (_dot(pooled.astype(BF16), pw_ref[g]) * ps_ref[:, cols]).astype(BF16)
    ext[0:C_HALO, :] = ext[tm:tm + C_HALO, :]

    dw = D_HEADS * HEAD_DIM
    for hd in range(D_HEADS):
        lanes = lambda base: slice(base + hd * LANES, base + (hd + 1) * LANES)
        qd_ref[hd] = (proj[:, lanes(C_WIDTH)] * (HEAD_DIM ** -0.5 * LOG2_E)).astype(BF16)
        kd_ref[hd] = proj[:, lanes(C_WIDTH + dw)].astype(BF16)
        vd_ref[hd] = proj[:, lanes(C_WIDTH + 2 * dw)].astype(BF16)


def _odd_proj(h, g, w_in, pool_w, pool_scale, *, seq, tm):
    t, d = h.shape
    row = lambda w: pl.BlockSpec((tm, w), lambda i: (i, 0))
    heads = pl.BlockSpec((D_HEADS, tm, LANES), lambda i: (0, i, 0))
    shp = jax.ShapeDtypeStruct((D_HEADS, t, LANES), BF16)
    return pl.pallas_call(
        functools.partial(_odd_proj_body, tm=tm, tiles_per_seq=seq // tm),
        grid=(t // tm,),
        in_specs=[row(d), _resident((1, d)), _resident(w_in.shape), _resident(pool_w.shape), _resident(pool_scale.shape)],
        out_specs=[row(C_WIDTH), heads, heads, heads],
        out_shape=[jax.ShapeDtypeStruct((t, C_WIDTH), BF16), shp, shp, shp],
        scratch_shapes=[pltpu.VMEM((C_HALO + tm, C_WIDTH), F32)],
        compiler_params=_params("arbitrary"),
        name="odd_proj",
    )(h, g, w_in, pool_w, pool_scale)


SB_SUB = 256


def _sb_body(q_ref, k_ref, v_ref, o_ref, z_a, z_b, e_a, e_b, carry_s, acc_s, *, tq, tk):
    qi = pl.program_id(2)
    n_blocks = 2 * qi + 2
    q = q_ref[0]
    carry_s[...] = jnp.zeros(carry_s.shape, F32)
    acc_s[...] = jnp.zeros(acc_s.shape, F32)
    later = (lax.broadcasted_iota(jnp.int32, (SB_SUB, SB_SUB), 0) > lax.broadcasted_iota(jnp.int32, (SB_SUB, SB_SUB), 1))
    later = jnp.where(later, 1.0, 0.0).astype(BF16)

    def key_rows(n):
        kb = jnp.maximum(n_blocks - 1 - n, 0)
        return pl.ds(pl.multiple_of(kb * tk, tk), tk)

    def logits(n, z_ref):
        z_ref[...] = _dot_nt(q, k_ref[0, key_rows(n), :])

    def gate(z_ref, e_ref, diag_block):
        carry = carry_s[...]
        for sub in reversed(range(tk // SB_SUB)):
            z = z_ref[:, sub * SB_SUB:(sub + 1) * SB_SUB]
            softplus = jnp.maximum(z, 0.0) + jnp.log2(1.0 + jnp.exp2(-jnp.abs(z)))
            arg = z - softplus
            if diag_block is None:
                keep = softplus
            else:
                row = lax.broadcasted_iota(jnp.int32, (tq, SB_SUB), 0)
                col = lax.broadcasted_iota(jnp.int32, (tq, SB_SUB), 1) + diag_block * tk + sub * SB_SUB
                keep = jnp.where(col < row, softplus, 0.0)
                arg = jnp.where(col < row, arg, MASK_VALUE)
            expo = arg - _dot(keep.astype(BF16), later)
            for c in range(SB_SUB // LANES):
                lo = sub * SB_SUB + c * LANES
                e_ref[:, lo:lo + LANES] = expo[:, c * LANES:(c + 1) * LANES] - carry
            carry = carry + jnp.sum(keep, axis=-1, keepdims=True)
        carry_s[...] = carry

    def weigh(e_ref, n):
        acc_s[...] += _dot(jnp.exp2(e_ref[...]).astype(BF16), v_ref[0, key_rows(n), :])

    logits(0, z_a)
    logits(1, z_b)
    gate(z_a, e_a, 1)
    logits(2, z_a)
    gate(z_b, e_b, 0)
    weigh(e_a, 0)

    def pair(u, carry):
        n = 2 * u + 2
        logits(n + 1, z_b)
        gate(z_a, e_a, None)
        weigh(e_b, n - 1)
        logits(n + 2, z_a)
        gate(z_b, e_b, None)
        weigh(e_a, n)
        return carry

    lax.fori_loop(0, qi, pair, 0)
    weigh(e_b, n_blocks - 1)
    o_ref[...] = acc_s[...].astype(BF16)


def _sb_attn(q, k, v, *, batch, seq, tq, tk):
    assert tq == 2 * tk and tk % SB_SUB == 0
    heads, t, _ = q.shape
    nq = seq // tq
    kv = pl.BlockSpec((1, seq, LANES), lambda h, b, i: (h, b, 0))
    return pl.pallas_call(
        functools.partial(_sb_body, tq=tq, tk=tk),
        grid=(heads, batch, nq),
        in_specs=[pl.BlockSpec((1, tq, LANES), lambda h, b, i: (h, b * nq + i, 0)), kv, kv],
        out_specs=pl.BlockSpec((tq, LANES), lambda h, b, i: (b * nq + i, h)),
        out_shape=jax.ShapeDtypeStruct((t, heads * LANES), BF16),
        scratch_shapes=[pltpu.VMEM((tq, tk), F32)] * 4 + [pltpu.VMEM((tq, LANES), F32)] * 2,
        compiler_params=_params("parallel", "parallel", "arbitrary"),
        name="sb_attn",
    )(q, k, v)


def _latent_head_lanes():
    half_pe, half_nope = B_ROPE // 2, B_NOPE // 2
    pe = np.concatenate([np.arange(half_pe), LANES // 2 + np.arange(half_pe)])
    nope = np.concatenate([half_pe + np.arange(half_nope), LANES // 2 + half_pe + np.arange(half_nope)])
    return pe, nope


def _scatter_cols(w, src_cols, dst_cols, width):
    out = jnp.zeros((w.shape[0], width), w.dtype)
    return out.at[:, np.asarray(dst_cols)].set(w[:, np.asarray(src_cols)])


def _even_weights(w_in, w_q_up, w_kv_up):
    pe, nope = _latent_head_lanes()
    a_w = A_HEADS * HEAD_DIM
    o_kr = 3 * a_w + B_Q_LORA + B_KV_LORA
    k_rope = _scatter_cols(w_in, o_kr + np.arange(B_ROPE), pe, LANES)
    w_in_p = jnp.concatenate([w_in[:, :o_kr], k_rope], axis=1).astype(BF16)
    qd, kvd = B_NOPE + B_ROPE, B_NOPE + B_V
    src_q, dst_q, src_k, dst_k, src_v = [], [], [], [], []
    for hd in range(B_HEADS):
        src_q += list(hd * qd + np.arange(B_NOPE)) + list(hd * qd + B_NOPE + np.arange(B_ROPE))
        dst_q += list(hd * LANES + nope) + list(hd * LANES + pe)
        src_k += list(hd * kvd + np.arange(B_NOPE))
        dst_k += list(hd * LANES + nope)
        src_v += list(hd * kvd + B_NOPE + np.arange(B_V))
    w_q = _scatter_cols(w_q_up, src_q, dst_q, B_HEADS * LANES).astype(BF16)
    w_k = _scatter_cols(w_kv_up, src_k, dst_k, B_HEADS * LANES)
    w_kv = jnp.concatenate([w_k, w_kv_up[:, np.asarray(src_v)]], axis=1).astype(BF16)
    return w_in_p, w_q, w_kv


def _rope_frequencies():
    inv_a = ROPE_THETA ** (-jnp.arange(0, HEAD_DIM, 2, dtype=F32) / HEAD_DIM)
    inv_b = ROPE_THETA ** (-jnp.arange(0, B_ROPE, 2, dtype=F32) / B_ROPE)
    pe, _ = _latent_head_lanes()
    inv_b_lanes = jnp.zeros((LANES,), F32).at[pe].set(jnp.concatenate([inv_b, inv_b]))
    return jnp.concatenate([inv_a, inv_a])[None, :], inv_b_lanes[None, :]


def kernel(x, positions, norm_g, ffn_w_gate, ffn_w_up, ffn_w_down, even_w_in, even_q_norm, even_w_q_up, even_kv_norm, even_w_kv_up, even_w_out, odd_w_in, odd_pool_w, odd_pool_scale, odd_w_out, final_norm):
    batch, seq, d = x.shape
    t = batch * seq
    assert seq % A_TILE == 0
    tm = 256
    h = x.reshape(t, d).astype(F32)
    pos = positions.reshape(t, 1).astype(jnp.int32)
    bf = lambda w: w.astype(BF16)
    gain = lambda i, j: norm_g[i, j][None, :].astype(F32)
    ffn_w = lambda i, j: (bf(ffn_w_gate[i, j]), bf(ffn_w_up[i, j]), bf(ffn_w_down[i, j]))

    h = _ffn(h, [], gain(0, 0), *ffn_w(0, 0), tm=tm)
    tabs = _rope_tables(pos, *_rope_frequencies(), tm=1024)
    w_in_p, w_q, w_kv = _even_weights(even_w_in[0], even_w_q_up[0], even_w_kv_up[0])
    qa, ka, va, qb, kb, vb = _even_proj(h, gain(0, 1), w_in_p, tabs, even_q_norm[0][None, :].astype(F32), w_q,
                                        even_kv_norm[0][None, :].astype(F32), w_kv, tm=tm)
    out_a = _dilated_attn(qa, ka, va, batch=batch, seq=seq)
    out_b = _mla_attn(qb, kb, vb, batch=batch, seq=seq, tq=1024, tk=512)
    a_w = A_HEADS * HEAD_DIM
    w_out = bf(even_w_out[0])
    h = _ffn(h, [(out_a, w_out[:a_w]), (out_b, w_out[a_w:])], gain(0, 2), *ffn_w(0, 1), tm=tm)

    h = _ffn(h, [], gain(1, 0), *ffn_w(1, 0), tm=tm)
    out_c, qd, kd, vd = _odd_proj(h, gain(1, 1), bf(odd_w_in[0]), bf(odd_pool_w[0]),
                                  odd_pool_scale[0][None, :].astype(F32), seq=seq, tm=tm)
    out_d = _sb_attn(qd, kd, vd, batch=batch, seq=seq, tq=1024, tk=512)
    w_out = bf(odd_w_out[0])
    h = _ffn(h, [(out_c, w_out[:C_WIDTH]), (out_d, w_out[C_WIDTH:])], gain(1, 2), *ffn_w(1, 1),
             final_norm[None, :].astype(F32), tm=tm)
    return h.reshape(batch, seq, d).astype(x.dtype)
```

```python
import functools

import jax
import jax.numpy as jnp
import numpy as np
from jax import lax
from jax.experimental import pallas as pl
from jax.experimental.pallas import tpu as pltpu

F32 = jnp.float32
BF16 = jnp.bfloat16

HEAD_DIM = 128
LANES = 128
ROPE_THETA = 10000.0
NORM_EPS = 1e-6
A_HEADS = 6
A_DILATIONS = (1, 4, 16)
A_BACK = 128
B_HEADS = 4
B_Q_LORA = 384
B_KV_LORA = 256
B_NOPE = 64
B_ROPE = 32
B_V = 128
C_WINDOWS = (2, 4, 8, 16)
C_GROUP_DIM = 128
C_WIDTH = len(C_WINDOWS) * C_GROUP_DIM
C_HALO = 16
D_HEADS = 4
MASK_VALUE = -1e30
LOG2_E = 1.4426950408889634
VMEM_LIMIT_BYTES = 56 * 1024 * 1024


def _params(*semantics):
    return pltpu.CompilerParams(dimension_semantics=semantics, vmem_limit_bytes=VMEM_LIMIT_BYTES)


def _resident(shape):
    return pl.BlockSpec(shape, lambda *_: (0,) * len(shape), pipeline_mode=pl.Buffered(1))


def _rms(x, g):
    ms = jnp.mean(x * x, axis=-1, keepdims=True)
    return x * lax.rsqrt(ms + NORM_EPS) * g


def _dot(a, b):
    return jnp.dot(a, b, preferred_element_type=F32)


def _dot_nt(a, b):
    return lax.dot_general(a, b, (((1,), (1,)), ((), ())), preferred_element_type=F32)


def _ffn_body(*refs, n_mix, final_norm):
    h_ref = refs[0]
    mix = [(refs[1 + 2 * i], refs[2 + 2 * i]) for i in range(n_mix)]
    base = 1 + 2 * n_mix
    g_ref, wg_ref, wu_ref, wd_ref = refs[base:base + 4]
    fg_ref = refs[base + 4] if final_norm else None
    o_ref = refs[-1]

    h = h_ref[...]
    for m_ref, w_ref in mix:
        h = h + _dot(m_ref[...], w_ref[...])
    xn = _rms(h, g_ref[...]).astype(BF16)
    gate = _dot(xn, wg_ref[...])
    up = _dot(xn, wu_ref[...])
    act = (gate * jax.nn.sigmoid(gate) * up).astype(BF16)
    y = h + 0.5 * _dot(act, wd_ref[...])
    if final_norm:
        y = _rms(y, fg_ref[...])
    o_ref[...] = y


def _ffn(h, mix, g, wg, wu, wd, final_g=None, *, tm):
    t, d = h.shape
    d_ff = wg.shape[1]
    row = lambda w: pl.BlockSpec((tm, w), lambda i: (i, 0))
    args, specs = [h], [row(d)]
    for m, w in mix:
        args += [m, w]
        specs += [row(m.shape[1]), _resident(w.shape)]
    args += [g, wg, wu, wd]
    specs += [_resident((1, d)), _resident((d, d_ff)), _resident((d, d_ff)), _resident((d_ff, d))]
    if final_g is not None:
        args.append(final_g)
        specs.append(_resident((1, d)))
    return pl.pallas_call(
        functools.partial(_ffn_body, n_mix=len(mix), final_norm=final_g is not None),
        grid=(t // tm,),
        in_specs=specs,
        out_specs=row(d),
        out_shape=jax.ShapeDtypeStruct((t, d), F32),
        compiler_params=_params("parallel"),
        name="ffn",
    )(*args)


def _rope_tables_body(pos_ref, inva_ref, invb_ref, ca_ref, sa_ref, cb_ref, sb_ref):
    pos = pos_ref[...].astype(F32)
    lane = lax.broadcasted_iota(jnp.int32, (1, LANES), 1)
    sign = jnp.where(lane < LANES // 2, -1.0, 1.0).astype(F32)
    ang_a = pos * inva_ref[...]
    ca_ref[...] = jnp.cos(ang_a)
    sa_ref[...] = sign * jnp.sin(ang_a)
    ang_b = pos * invb_ref[...]
    cb_ref[...] = jnp.cos(ang_b)
    sb_ref[...] = sign * jnp.sin(ang_b)


def _rope_tables(pos, inv_a, inv_b, *, tm):
    t = pos.shape[0]
    tab = pl.BlockSpec((tm, LANES), lambda i: (i, 0))
    return pl.pallas_call(
        _rope_tables_body,
        grid=(t // tm,),
        in_specs=[pl.BlockSpec((tm, 1), lambda i: (i, 0)), _resident((1, LANES)), _resident((1, LANES))],
        out_specs=[tab] * 4,
        out_shape=[jax.ShapeDtypeStruct((t, LANES), F32)] * 4,
        compiler_params=_params("parallel"),
        name="rope_tables",
    )(pos, inv_a, inv_b)


def _rotate(x, c, s):
    return x * c + pltpu.roll(x, LANES // 2, 1) * s


def _even_proj_body(h_ref, g_ref, win_ref, ca_ref, sa_ref, cb_ref, sb_ref, qn_ref, wq_ref, kvn_ref, wkv_ref,
                    qa_ref, ka_ref, va_ref, qb_ref, kb_ref, vb_ref):
    xn = _rms(h_ref[...], g_ref[...]).astype(BF16)
    proj = _dot(xn, win_ref[...])
    ca, sa, cb, sb = ca_ref[...], sa_ref[...], cb_ref[...], sb_ref[...]
    a_w = A_HEADS * HEAD_DIM
    blk = lambda base, i: slice(base + i * LANES, base + (i + 1) * LANES)
    for hd in range(A_HEADS):
        qa_ref[hd] = (_rotate(proj[:, blk(0, hd)], ca, sa) * (HEAD_DIM ** -0.5)).astype(BF16)
        ka_ref[hd] = _rotate(proj[:, blk(a_w, hd)], ca, sa).astype(BF16)
        va_ref[hd] = proj[:, blk(2 * a_w, hd)].astype(BF16)
    o_cq = 3 * a_w
    o_ckv = o_cq + B_Q_LORA
    o_kr = o_ckv + B_KV_LORA
    cq = _rms(proj[:, o_cq:o_ckv], qn_ref[...]).astype(BF16)
    ckv = _rms(proj[:, o_ckv:o_kr], kvn_ref[...]).astype(BF16)
    qb = _dot(cq, wq_ref[...])
    kv = _dot(ckv, wkv_ref[...])
    k_pe = _rotate(proj[:, o_kr:o_kr + LANES], cb, sb)
    q_scale = (B_NOPE + B_ROPE) ** -0.5 * LOG2_E
    for hd in range(B_HEADS):
        qb_ref[hd] = (_rotate(qb[:, blk(0, hd)], cb, sb) * q_scale).astype(BF16)
        kb_ref[hd] = (kv[:, blk(0, hd)] + k_pe).astype(BF16)
        vb_ref[hd] = kv[:, blk(B_HEADS * LANES, hd)].astype(BF16)


def _even_proj(h, g, w_in, tabs, q_norm, w_q, kv_norm, w_kv, *, tm):
    t, d = h.shape
    row = lambda w: pl.BlockSpec((tm, w), lambda i: (i, 0))
    heads = lambda n: pl.BlockSpec((n, tm, LANES), lambda i: (0, i, 0))
    shp = lambda n: jax.ShapeDtypeStruct((n, t, LANES), BF16)
    return pl.pallas_call(
        _even_proj_body,
        grid=(t // tm,),
        in_specs=[row(d), _resident((1, d)), _resident(w_in.shape)] + [row(LANES)] * 4
        + [_resident(q_norm.shape), _resident(w_q.shape), _resident(kv_norm.shape), _resident(w_kv.shape)],
        out_specs=[heads(A_HEADS)] * 3 + [heads(B_HEADS)] * 3,
        out_shape=[shp(A_HEADS)] * 3 + [shp(B_HEADS)] * 3,
        compiler_params=_params("parallel"),
        name="even_proj",
    )(h, g, w_in, *tabs, q_norm, w_q, kv_norm, w_kv)


A_TILE = A_BACK * max(A_DILATIONS)


def _dilated_body(q_ref, k_ref, v_ref, o_ref, qf, kf, vf, o_s, lse_s):
    i = pl.program_id(2)

    @pl.when(i > 0)
    def _():
        kf[0:A_TILE] = kf[A_TILE:2 * A_TILE]
        vf[0:A_TILE] = vf[A_TILE:2 * A_TILE]

    @pl.when(i == 0)
    def _():
        kf[0:A_TILE] = jnp.zeros((A_TILE, HEAD_DIM), F32)
        vf[0:A_TILE] = jnp.zeros((A_TILE, HEAD_DIM), F32)

    kf[A_TILE:2 * A_TILE] = k_ref[0].astype(F32)
    vf[A_TILE:2 * A_TILE] = v_ref[0].astype(F32)
    qf[...] = q_ref[0].astype(F32)

    qi = lax.broadcasted_iota(jnp.int32, (A_BACK, 2 * A_BACK), 0)
    ki = lax.broadcasted_iota(jnp.int32, (A_BACK, 2 * A_BACK), 1)
    band = (ki >= qi) & (ki <= qi + A_BACK)
    band_first = band & (ki >= jnp.where(i > 0, 0, A_BACK))

    for p, dil in enumerate(A_DILATIONS):
        for r in range(dil):
            for j in range(max(A_DILATIONS) // dil):
                q_rows = pl.ds(r + j * A_BACK * dil, A_BACK, stride=dil)
                kv_rows = pl.ds(A_TILE + (j - 1) * A_BACK * dil + r, 2 * A_BACK, stride=dil)
                qb = qf[q_rows, :].astype(BF16)
                kb = kf[kv_rows, :].astype(BF16)
                vb = vf[kv_rows, :].astype(BF16)
                s = jnp.where(band_first if j == 0 else band, _dot_nt(qb, kb), MASK_VALUE)
                m = jnp.max(s, axis=-1, keepdims=True)
                e = jnp.exp(s - m)
                den = jnp.sum(e, axis=-1, keepdims=True)
                o_s[p, q_rows, :] = _dot(e.astype(BF16), vb) / den
                lse_s[p, q_rows, :] = jnp.broadcast_to(m + jnp.log(den), (A_BACK, HEAD_DIM))

    n_pat = len(A_DILATIONS)
    lse = [lse_s[p] for p in range(n_pat)]
    top = functools.reduce(jnp.maximum, lse)
    wts = [jnp.exp(x - top) for x in lse]
    total = functools.reduce(jnp.add, wts)
    out = functools.reduce(jnp.add, [w * o_s[p] for p, w in enumerate(wts)]) / total
    o_ref[...] = out.astype(BF16)


def _dilated_attn(q, k, v, *, batch, seq):
    n_tiles = seq // A_TILE
    heads, t, _ = q.shape
    blk = pl.BlockSpec((1, A_TILE, HEAD_DIM), lambda b, h, i: (h, b * n_tiles + i, 0))
    return pl.pallas_call(
        _dilated_body,
        grid=(batch, heads, n_tiles),
        in_specs=[blk, blk, blk],
        out_specs=pl.BlockSpec((A_TILE, HEAD_DIM), lambda b, h, i: (b * n_tiles + i, h)),
        out_shape=jax.ShapeDtypeStruct((t, heads * HEAD_DIM), BF16),
        scratch_shapes=[
            pltpu.VMEM((A_TILE, HEAD_DIM), F32),
            pltpu.VMEM((2 * A_TILE, HEAD_DIM), F32),
            pltpu.VMEM((2 * A_TILE, HEAD_DIM), F32),
            pltpu.VMEM((len(A_DILATIONS), A_TILE, HEAD_DIM), F32),
            pltpu.VMEM((len(A_DILATIONS), A_TILE, HEAD_DIM), F32),
        ],
        compiler_params=_params("parallel", "parallel", "arbitrary"),
        name="dilated_attn",
    )(q, k, v)


def _mla_body(q_ref, k_ref, v_ref, o_ref, s_a, s_b, m_s, l_s, acc_s, *, tq, tk):
    qi = pl.program_id(2)
    ratio = tq // tk
    n_tiles = tk // LANES
    q = q_ref[0]
    m_s[...] = jnp.full(m_s.shape, MASK_VALUE, F32)
    l_s[...] = jnp.zeros(l_s.shape, F32)
    acc_s[...] = jnp.zeros(acc_s.shape, F32)

    def key_rows(kb):
        return pl.ds(pl.multiple_of(kb * tk, tk), tk)

    def logits(kb, s_ref):
        s_ref[...] = _dot_nt(q, k_ref[0, key_rows(kb), :])

    def softmax_av(s_ref, kb, diag_block):
        tiles = [s_ref[:, c * LANES:(c + 1) * LANES] for c in range(n_tiles)]
        if diag_block is not None:
            row = lax.broadcasted_iota(jnp.int32, (tq, LANES), 0)
            col = lax.broadcasted_iota(jnp.int32, (tq, LANES), 1) + diag_block * tk
            tiles = [jnp.where(col + c * LANES <= row, x, MASK_VALUE) for c, x in enumerate(tiles)]
        m_prev = m_s[...]
        m_new = jnp.maximum(m_prev, jnp.max(functools.reduce(jnp.maximum, tiles), axis=-1, keepdims=True))
        alpha = jnp.exp2(m_prev - m_new)
        p = [jnp.exp2(x - m_new) for x in tiles]
        l_s[...] = alpha * l_s[...] + functools.reduce(jnp.add, p)
        pb = jnp.concatenate([x.astype(BF16) for x in p], axis=1)
        acc_s[...] = alpha * acc_s[...] + _dot(pb, v_ref[0, key_rows(kb), :])
        m_s[...] = m_new

    first_diag = qi * ratio
    logits(0, s_a)

    def pair(u, carry):
        logits(2 * u + 1, s_b)
        softmax_av(s_a, 2 * u, None)
        logits(2 * u + 2, s_a)
        softmax_av(s_b, 2 * u + 1, None)
        return carry

    lax.fori_loop(0, first_diag // 2, pair, 0)
    logits(first_diag + 1, s_b)
    softmax_av(s_a, first_diag, 0)
    softmax_av(s_b, first_diag + 1, 1)
    o_ref[...] = (acc_s[...] / jnp.sum(l_s[...], axis=-1, keepdims=True)).astype(BF16)


def _mla_attn(q, k, v, *, batch, seq, tq, tk):
    assert tq == 2 * tk
    heads, t, _ = q.shape
    nq = seq // tq
    kv = pl.BlockSpec((1, seq, LANES), lambda h, b, i: (h, b, 0))
    return pl.pallas_call(
        functools.partial(_mla_body, tq=tq, tk=tk),
        grid=(heads, batch, nq),
        in_specs=[pl.BlockSpec((1, tq, LANES), lambda h, b, i: (h, b * nq + i, 0)), kv, kv],
        out_specs=pl.BlockSpec((tq, LANES), lambda h, b, i: (b * nq + i, h)),
        out_shape=jax.ShapeDtypeStruct((t, heads * LANES), BF16),
        scratch_shapes=[pltpu.VMEM((tq, tk), F32), pltpu.VMEM((tq, tk), F32), pltpu.VMEM((tq, LANES), F32),
                        pltpu.VMEM((tq, LANES), F32), pltpu.VMEM((tq, LANES), F32)],
        compiler_params=_params("parallel", "parallel", "arbitrary"),
        name="mla_attn",
    )(q, k, v)


def _odd_proj_body(h_ref, g_ref, win_ref, pw_ref, ps_ref, oc_ref, qd_ref, kd_ref, vd_ref, ext, *, tm, tiles_per_seq):
    i = pl.program_id(0)
    xn = _rms(h_ref[...], g_ref[...]).astype(BF16)
    proj = _dot(xn, win_ref[...])

    @pl.when(i % tiles_per_seq == 0)
    def _():
        ext[0:C_HALO, :] = jnp.zeros((C_HALO, C_WIDTH), F32)

    ext[C_HALO:C_HALO + tm, :] = proj[:, 0:C_WIDTH]
    t_in_seq = (i % tiles_per_seq) * tm + lax.broadcasted_iota(jnp.int32, (tm, 1), 0)
    for g, w in enumerate(C_WINDOWS):
        cols = slice(g * C_GROUP_DIM, (g + 1) * C_GROUP_DIM)
        s = ext[C_HALO - (w - 1):C_HALO + tm, cols]
        step = 1
        while step < w:
            s = s[step:] + s[:-step]
            step *= 2
        count = jnp.minimum(t_in_seq + 1, w).astype(F32)
        pooled = s / count - proj[:, cols]
        oc_ref[:, cols] = (_dot(pooled.astype(BF16), pw_ref[g]) * ps_ref[:, cols]).astype(BF16)
    ext[0:C_HALO, :] = ext[tm:tm + C_HALO, :]

    dw = D_HEADS * HEAD_DIM
    for hd in range(D_HEADS):
        lanes = lambda base: slice(base + hd * LANES, base + (hd + 1) * LANES)
        qd_ref[hd] = (proj[:, lanes(C_WIDTH)] * (HEAD_DIM ** -0.5 * LOG2_E)).astype(BF16)
        kd_ref[hd] = proj[:, lanes(C_WIDTH + dw)].astype(BF16)
        vd_ref[hd] = proj[:, lanes(C_WIDTH + 2 * dw)].astype(BF16)


def _odd_proj(h, g, w_in, pool_w, pool_scale, *, seq, tm):
    t, d = h.shape
    row = lambda w: pl.BlockSpec((tm, w), lambda i: (i, 0))
    heads = pl.BlockSpec((D_HEADS, tm, LANES), lambda i: (0, i, 0))
    shp = jax.ShapeDtypeStruct((D_HEADS, t, LANES), BF16)
    return pl.pallas_call(
        functools.partial(_odd_proj_body, tm=tm, tiles_per_seq=seq // tm),
        grid=(t // tm,),
        in_specs=[row(d), _resident((1, d)), _resident(w_in.shape), _resident(pool_w.shape), _resident(pool_scale.shape)],
        out_specs=[row(C_WIDTH), heads, heads, heads],
        out_shape=[jax.ShapeDtypeStruct((t, C_WIDTH), BF16), shp, shp, shp],
        scratch_shapes=[pltpu.VMEM((C_HALO + tm, C_WIDTH), F32)],
        compiler_params=_params("arbitrary"),
        name="odd_proj",
    )(h, g, w_in, pool_w, pool_scale)


SB_SUB = 256
SB_DEAD_CARRY = 160.0


def _sb_body(q_ref, k_ref, v_ref, o_ref, z_a, z_b, e_a, e_b, carry_s, acc_s, *, tq, tk):
    qi = pl.program_id(2)
    n_blocks = 2 * qi + 2
    q = q_ref[0]
    carry_s[...] = jnp.zeros(carry_s.shape, F32)
    acc_s[...] = jnp.zeros(acc_s.shape, F32)
    later = (lax.broadcasted_iota(jnp.int32, (SB_SUB, SB_SUB), 0) > lax.broadcasted_iota(jnp.int32, (SB_SUB, SB_SUB), 1))
    later = jnp.where(later, 1.0, 0.0).astype(BF16)

    def key_rows(n):
        kb = jnp.maximum(n_blocks - 1 - n, 0)
        return pl.ds(pl.multiple_of(kb * tk, tk), tk)

    def logits(n, z_ref):
        z_ref[...] = _dot_nt(q, k_ref[0, key_rows(n), :])

    def gate(z_ref, e_ref, diag_block):
        carry = carry_s[...]
        for sub in reversed(range(tk // SB_SUB)):
            z = z_ref[:, sub * SB_SUB:(sub + 1) * SB_SUB]
            softplus = jnp.maximum(z, 0.0) + jnp.log2(1.0 + jnp.exp2(-jnp.abs(z)))
            arg = z - softplus
            if diag_block is None:
                keep = softplus
            else:
                row = lax.broadcasted_iota(jnp.int32, (tq, SB_SUB), 0)
                col = lax.broadcasted_iota(jnp.int32, (tq, SB_SUB), 1) + diag_block * tk + sub * SB_SUB
                keep = jnp.where(col < row, softplus, 0.0)
                arg = jnp.where(col < row, arg, MASK_VALUE)
            expo = arg - _dot(keep.astype(BF16), later)
            for c in range(SB_SUB // LANES):
                lo = sub * SB_SUB + c * LANES
                e_ref[:, lo:lo + LANES] = expo[:, c * LANES:(c + 1) * LANES] - carry
            carry = carry + jnp.sum(keep, axis=-1, keepdims=True)
        carry_s[...] = carry

    def weigh(e_ref, n):
        acc_s[...] += _dot(jnp.exp2(e_ref[...]).astype(BF16), v_ref[0, key_rows(n), :])

    logits(0, z_a)
    logits(1, z_b)
    gate(z_a, e_a, 1)
    logits(2, z_a)
    gate(z_b, e_b, 0)
    weigh(e_a, 0)

    def pair(state):
        u, _ = state
        n = 2 * u + 2
        logits(n + 1, z_b)
        gate(z_a, e_a, None)
        weigh(e_b, n - 1)
        logits(n + 2, z_a)
        gate(z_b, e_b, None)
        weigh(e_a, n)
        return u + 1, jnp.min(carry_s[...])

    def more_blocks(state):
        u, min_carry = state
        return (u < qi) & (min_carry < SB_DEAD_CARRY)

    pairs_done, _ = lax.while_loop(more_blocks, pair, (jnp.int32(0), jnp.min(carry_s[...])))
    weigh(e_b, 2 * pairs_done + 1)
    o_ref[...] = acc_s[...].astype(BF16)


def _sb_attn(q, k, v, *, batch, seq, tq, tk):
    assert tq == 2 * tk and tk % SB_SUB == 0
    heads, t, _ = q.shape
    nq = seq // tq
    kv = pl.BlockSpec((1, seq, LANES), lambda h, b, i: (h, b, 0))
    return pl.pallas_call(
        functools.partial(_sb_body, tq=tq, tk=tk),
        grid=(heads, batch, nq),
        in_specs=[pl.BlockSpec((1, tq, LANES), lambda h, b, i: (h, b * nq + i, 0)), kv, kv],
        out_specs=pl.BlockSpec((tq, LANES), lambda h, b, i: (b * nq + i, h)),
        out_shape=jax.ShapeDtypeStruct((t, heads * LANES), BF16),
        scratch_shapes=[pltpu.VMEM((tq, tk), F32)] * 4 + [pltpu.VMEM((tq, LANES), F32)] * 2,
        compiler_params=_params("parallel", "parallel", "arbitrary"),
        name="sb_attn",
    )(q, k, v)


def _latent_head_lanes():
    half_pe, half_nope = B_ROPE // 2, B_NOPE // 2
    pe = np.concatenate([np.arange(half_pe), LANES // 2 + np.arange(half_pe)])
    nope = np.concatenate([half_pe + np.arange(half_nope), LANES // 2 + half_pe + np.arange(half_nope)])
    return pe, nope


def _scatter_cols(w, src_cols, dst_cols, width):
    out = jnp.zeros((w.shape[0], width), w.dtype)
    return out.at[:, np.asarray(dst_cols)].set(w[:, np.asarray(src_cols)])


def _even_weights(w_in, w_q_up, w_kv_up):
    pe, nope = _latent_head_lanes()
    a_w = A_HEADS * HEAD_DIM
    o_kr = 3 * a_w + B_Q_LORA + B_KV_LORA
    k_rope = _scatter_cols(w_in, o_kr + np.arange(B_ROPE), pe, LANES)
    w_in_p = jnp.concatenate([w_in[:, :o_kr], k_rope], axis=1).astype(BF16)
    qd, kvd = B_NOPE + B_ROPE, B_NOPE + B_V
    src_q, dst_q, src_k, dst_k, src_v = [], [], [], [], []
    for hd in range(B_HEADS):
        src_q += list(hd * qd + np.arange(B_NOPE)) + list(hd * qd + B_NOPE + np.arange(B_ROPE))
        dst_q += list(hd * LANES + nope) + list(hd * LANES + pe)
        src_k += list(hd * kvd + np.arange(B_NOPE))
        dst_k += list(hd * LANES + nope)
        src_v += list(hd * kvd + B_NOPE + np.arange(B_V))
    w_q = _scatter_cols(w_q_up, src_q, dst_q, B_HEADS * LANES).astype(BF16)
    w_k = _scatter_cols(w_kv_up, src_k, dst_k, B_HEADS * LANES)
    w_kv = jnp.concatenate([w_k, w_kv_up[:, np.asarray(src_v)]], axis=1).astype(BF16)
    return w_in_p, w_q, w_kv


def _rope_frequencies():
    inv_a = ROPE_THETA ** (-jnp.arange(0, HEAD_DIM, 2, dtype=F32) / HEAD_DIM)
    inv_b = ROPE_THETA ** (-jnp.arange(0, B_ROPE, 2, dtype=F32) / B_ROPE)
    pe, _ = _latent_head_lanes()
    inv_b_lanes = jnp.zeros((LANES,), F32).at[pe].set(jnp.concatenate([inv_b, inv_b]))
    return jnp.concatenate([inv_a, inv_a])[None, :], inv_b_lanes[None, :]


def kernel(x, positions, norm_g, ffn_w_gate, ffn_w_up, ffn_w_down, even_w_in, even_q_norm, even_w_q_up, even_kv_norm, even_w_kv_up, even_w_out, odd_w_in, odd_pool_w, odd_pool_scale, odd_w_out, final_norm):
    batch, seq, d = x.shape
    t = batch * seq
    assert seq % A_TILE == 0
    tm = 256
    h = x.reshape(t, d).astype(F32)
    pos = positions.reshape(t, 1).astype(jnp.int32)
    bf = lambda w: w.astype(BF16)
    gain = lambda i, j: norm_g[i, j][None, :].astype(F32)
    ffn_w = lambda i, j: (bf(ffn_w_gate[i, j]), bf(ffn_w_up[i, j]), bf(ffn_w_down[i, j]))

    h = _ffn(h, [], gain(0, 0), *ffn_w(0, 0), tm=tm)
    tabs = _rope_tables(pos, *_rope_frequencies(), tm=1024)
    w_in_p, w_q, w_kv = _even_weights(even_w_in[0], even_w_q_up[0], even_w_kv_up[0])
    qa, ka, va, qb, kb, vb = _even_proj(h, gain(0, 1), w_in_p, tabs, even_q_norm[0][None, :].astype(F32), w_q,
                                        even_kv_norm[0][None, :].astype(F32), w_kv, tm=tm)
    out_a = _dilated_attn(qa, ka, va, batch=batch, seq=seq)
    out_b = _mla_attn(qb, kb, vb, batch=batch, seq=seq, tq=1024, tk=512)
    a_w = A_HEADS * HEAD_DIM
    w_out = bf(even_w_out[0])
    h = _ffn(h, [(out_a, w_out[:a_w]), (out_b, w_out[a_w:])], gain(0, 2), *ffn_w(0, 1), tm=tm)

    h = _ffn(h, [], gain(1, 0), *ffn_w(1, 0), tm=tm)
    out_c, qd, kd, vd = _odd_proj(h, gain(1, 1), bf(odd_w_in[0]), bf(odd_pool_w[0]),
                                  odd_pool_scale[0][None, :].astype(F32), seq=seq, tm=tm)
    out_d = _sb_attn(qd, kd, vd, batch=batch, seq=seq, tq=512, tk=256)
    w_out = bf(odd_w_out[0])
    h = _ffn(h, [(out_c, w_out[:C_WIDTH]), (out_d, w_out[C_WIDTH:])], gain(1, 2), *ffn_w(1, 1),
             final_norm[None, :].astype(F32), tm=tm)
    return h.reshape(batch, seq, d).astype(x.dtype)
```

```python
import functools

import jax
import jax.numpy as jnp
import numpy as np
from jax import lax
from jax.experimental import pallas as pl
from jax.experimental.pallas import tpu as pltpu

F32 = jnp.float32
BF16 = jnp.bfloat16

HEAD_DIM = 128
LANES = 128
ROPE_THETA = 10000.0
NORM_EPS = 1e-6
A_HEADS = 6
A_DILATIONS = (1, 4, 16)
A_BACK = 128
B_HEADS = 4
B_Q_LORA = 384
B_KV_LORA = 256
B_NOPE = 64
B_ROPE = 32
B_V = 128
C_WINDOWS = (2, 4, 8, 16)
C_GROUP_DIM = 128
C_WIDTH = len(C_WINDOWS) * C_GROUP_DIM
C_HALO = 16
D_HEADS = 4
MASK_VALUE = -1e30
LOG2_E = 1.4426950408889634
VMEM_LIMIT_BYTES = 56 * 1024 * 1024


def _params(*semantics):
    return pltpu.CompilerParams(dimension_semantics=semantics, vmem_limit_bytes=VMEM_LIMIT_BYTES)


def _resident(shape):
    return pl.BlockSpec(shape, lambda *_: (0,) * len(shape), pipeline_mode=pl.Buffered(1))


def _rms(x, g):
    ms = jnp.mean(x * x, axis=-1, keepdims=True)
    return x * lax.rsqrt(ms + NORM_EPS) * g


def _dot(a, b):
    return jnp.dot(a, b, preferred_element_type=F32)


def _dot_nt(a, b):
    return lax.dot_general(a, b, (((1,), (1,)), ((), ())), preferred_element_type=F32)


def _ffn_body(*refs, n_mix, final_norm):
    h_ref = refs[0]
    mix = [(refs[1 + 2 * i], refs[2 + 2 * i]) for i in range(n_mix)]
    base = 1 + 2 * n_mix
    g_ref, wg_ref, wu_ref, wd_ref = refs[base:base + 4]
    fg_ref = refs[base + 4] if final_norm else None
    o_ref = refs[-1]

    h = h_ref[...]
    for m_ref, w_ref in mix:
        h = h + _dot(m_ref[...], w_ref[...])
    xn = _rms(h, g_ref[...]).astype(BF16)
    gate = _dot(xn, wg_ref[...])
    up = _dot(xn, wu_ref[...])
    act = (gate * jax.nn.sigmoid(gate) * up).astype(BF16)
    y = h + 0.5 * _dot(act, wd_ref[...])
    if final_norm:
        y = _rms(y, fg_ref[...])
    o_ref[...] = y


def _ffn(h, mix, g, wg, wu, wd, final_g=None, *, tm):
    t, d = h.shape
    d_ff = wg.shape[1]
    row = lambda w: pl.BlockSpec((tm, w), lambda i: (i, 0))
    args, specs = [h], [row(d)]
    for m, w in mix:
        args += [m, w]
        specs += [row(m.shape[1]), _resident(w.shape)]
    args += [g, wg, wu, wd]
    specs += [_resident((1, d)), _resident((d, d_ff)), _resident((d, d_ff)), _resident((d_ff, d))]
    if final_g is not None:
        args.append(final_g)
        specs.append(_resident((1, d)))
    return pl.pallas_call(
        functools.partial(_ffn_body, n_mix=len(mix), final_norm=final_g is not None),
        grid=(t // tm,),
        in_specs=specs,
        out_specs=row(d),
        out_shape=jax.ShapeDtypeStruct((t, d), F32),
        compiler_params=_params("parallel"),
        name="ffn",
    )(*args)


def _rope_tables_body(pos_ref, inva_ref, invb_ref, ca_ref, sa_ref, cb_ref, sb_ref):
    pos = pos_ref[...].astype(F32)
    lane = lax.broadcasted_iota(jnp.int32, (1, LANES), 1)
    sign = jnp.where(lane < LANES // 2, -1.0, 1.0).astype(F32)
    ang_a = pos * inva_ref[...]
    ca_ref[...] = jnp.cos(ang_a)
    sa_ref[...] = sign * jnp.sin(ang_a)
    ang_b = pos * invb_ref[...]
    cb_ref[...] = jnp.cos(ang_b)
    sb_ref[...] = sign * jnp.sin(ang_b)


def _rope_tables(pos, inv_a, inv_b, *, tm):
    t = pos.shape[0]
    tab = pl.BlockSpec((tm, LANES), lambda i: (i, 0))
    return pl.pallas_call(
        _rope_tables_body,
        grid=(t // tm,),
        in_specs=[pl.BlockSpec((tm, 1), lambda i: (i, 0)), _resident((1, LANES)), _resident((1, LANES))],
        out_specs=[tab] * 4,
        out_shape=[jax.ShapeDtypeStruct((t, LANES), F32)] * 4,
        compiler_params=_params("parallel"),
        name="rope_tables",
    )(pos, inv_a, inv_b)


def _rotate(x, c, s):
    return x * c + pltpu.roll(x, LANES // 2, 1) * s


def _even_proj_body(h_ref, g_ref, win_ref, ca_ref, sa_ref, cb_ref, sb_ref, qn_ref, wq_ref, kvn_ref, wkv_ref,
                    qa_ref, ka_ref, va_ref, qb_ref, kb_ref, vb_ref):
    xn = _rms(h_ref[...], g_ref[...]).astype(BF16)
    proj = _dot(xn, win_ref[...])
    ca, sa, cb, sb = ca_ref[...], sa_ref[...], cb_ref[...], sb_ref[...]
    a_w = A_HEADS * HEAD_DIM
    blk = lambda base, i: slice(base + i * LANES, base + (i + 1) * LANES)
    for hd in range(A_HEADS):
        qa_ref[hd] = (_rotate(proj[:, blk(0, hd)], ca, sa) * (HEAD_DIM ** -0.5 * LOG2_E)).astype(BF16)
        ka_ref[hd] = _rotate(proj[:, blk(a_w, hd)], ca, sa).astype(BF16)
        va_ref[hd] = proj[:, blk(2 * a_w, hd)].astype(BF16)
    o_cq = 3 * a_w
    o_ckv = o_cq + B_Q_LORA
    o_kr = o_ckv + B_KV_LORA
    cq = _rms(proj[:, o_cq:o_ckv], qn_ref[...]).astype(BF16)
    ckv = _rms(proj[:, o_ckv:o_kr], kvn_ref[...]).astype(BF16)
    qb = _dot(cq, wq_ref[...])
    kv = _dot(ckv, wkv_ref[...])
    k_pe = _rotate(proj[:, o_kr:o_kr + LANES], cb, sb)
    q_scale = (B_NOPE + B_ROPE) ** -0.5 * LOG2_E
    for hd in range(B_HEADS):
        qb_ref[hd] = (_rotate(qb[:, blk(0, hd)], cb, sb) * q_scale).astype(BF16)
        kb_ref[hd] = (kv[:, blk(0, hd)] + k_pe).astype(BF16)
        vb_ref[hd] = kv[:, blk(B_HEADS * LANES, hd)].astype(BF16)


def _even_proj(h, g, w_in, tabs, q_norm, w_q, kv_norm, w_kv, *, tm):
    t, d = h.shape
    row = lambda w: pl.BlockSpec((tm, w), lambda i: (i, 0))
    heads = lambda n: pl.BlockSpec((n, tm, LANES), lambda i: (0, i, 0))
    shp = lambda n: jax.ShapeDtypeStruct((n, t, LANES), BF16)
    return pl.pallas_call(
        _even_proj_body,
        grid=(t // tm,),
        in_specs=[row(d), _resident((1, d)), _resident(w_in.shape)] + [row(LANES)] * 4
        + [_resident(q_norm.shape), _resident(w_q.shape), _resident(kv_norm.shape), _resident(w_kv.shape)],
        out_specs=[heads(A_HEADS)] * 3 + [heads(B_HEADS)] * 3,
        out_shape=[shp(A_HEADS)] * 3 + [shp(B_HEADS)] * 3,
        compiler_params=_params("parallel"),
        name="even_proj",
    )(h, g, w_in, *tabs, q_norm, w_q, kv_norm, w_kv)


A_TILE = A_BACK * max(A_DILATIONS)
A_STEP = 4
assert A_DILATIONS == (1, A_STEP, A_STEP * A_STEP)
A_PHASE4 = A_TILE // A_STEP


def _dilated_body(q_ref, k_ref, v_ref, o_ref, nat, p4f, q4, q16, k1, k4, k16, v1, v4, v16, o_s, lse_s, bias):
    i = pl.program_id(2)

    @pl.when(i > 0)
    def _():
        for buf in (k1, v1):
            buf[0:A_BACK] = buf[A_TILE:A_TILE + A_BACK]
        for buf in (k4, v4):
            buf[:, 0:A_BACK] = buf[:, A_PHASE4:A_PHASE4 + A_BACK]
        for buf in (k16, v16):
            buf[:, 0:A_BACK] = buf[:, A_BACK:2 * A_BACK]

    @pl.when(i == 0)
    def _():
        for buf in (k1, v1):
            buf[0:A_BACK] = jnp.zeros((A_BACK, HEAD_DIM), BF16)
        for buf in (k4, v4, k16, v16):
            buf[:, 0:A_BACK] = jnp.zeros((buf.shape[0], A_BACK, HEAD_DIM), BF16)

    def regroup(x_ref, d4, d16, halo):
        nat[...] = x_ref[0].astype(F32)
        for r4 in range(A_STEP):
            rows = nat[pl.ds(r4, A_PHASE4, stride=A_STEP), :]
            p4f[r4 * A_PHASE4:(r4 + 1) * A_PHASE4] = rows
            d4[r4, halo:halo + A_PHASE4] = rows.astype(BF16)
        for r16 in range(A_STEP * A_STEP):
            r4, sub = r16 % A_STEP, r16 // A_STEP
            rows = p4f[pl.ds(r4 * A_PHASE4 + sub, A_BACK, stride=A_STEP), :]
            d16[r16, halo:halo + A_BACK] = rows.astype(BF16)

    regroup(q_ref, q4, q16, 0)
    k1[A_BACK:A_BACK + A_TILE] = k_ref[0]
    regroup(k_ref, k4, k16, A_BACK)
    v1[A_BACK:A_BACK + A_TILE] = v_ref[0]
    regroup(v_ref, v4, v16, A_BACK)

    qi = lax.broadcasted_iota(jnp.int32, (A_BACK, 2 * A_BACK), 0)
    ki = lax.broadcasted_iota(jnp.int32, (A_BACK, 2 * A_BACK), 1)
    band = (ki >= qi) & (ki <= qi + A_BACK)
    bias[0] = jnp.where(band, 0.0, MASK_VALUE)
    bias[1] = jnp.where(band & (ki >= jnp.where(i > 0, 0, A_BACK)), 0.0, MASK_VALUE)

    def band_block(p, qb, kb, vb, first, out_rows):
        s = _dot_nt(qb, kb) + bias[1 if first else 0]
        m = jnp.max(s, axis=-1, keepdims=True)
        e = jnp.exp2(s - m)
        den = jnp.sum(e, axis=-1, keepdims=True)
        o_s[p, out_rows, :] = _dot(e.astype(BF16), vb) / den
        lse_s[p, out_rows, :] = jnp.broadcast_to(m + jnp.log2(den), (A_BACK, HEAD_DIM))

    for j in range(A_TILE // A_BACK):
        blk, bnd = slice(j * A_BACK, (j + 1) * A_BACK), slice(j * A_BACK, (j + 2) * A_BACK)
        band_block(0, q_ref[0, blk, :], k1[bnd], v1[bnd], j == 0, blk)
    for r4 in range(A_STEP):
        for j in range(A_PHASE4 // A_BACK):
            blk, bnd = slice(j * A_BACK, (j + 1) * A_BACK), slice(j * A_BACK, (j + 2) * A_BACK)
            band_block(1, q4[r4, blk], k4[r4, bnd], v4[r4, bnd], j == 0,
                       pl.ds(r4 + j * A_BACK * A_STEP, A_BACK, stride=A_STEP))
    for r16 in range(A_STEP * A_STEP):
        band_block(2, q16[r16], k16[r16], v16[r16], True, pl.ds(r16, A_BACK, stride=A_STEP * A_STEP))

    n_pat = len(A_DILATIONS)
    lse = [lse_s[p] for p in range(n_pat)]
    top = functools.reduce(jnp.maximum, lse)
    wts = [jnp.exp2(x - top) for x in lse]
    total = functools.reduce(jnp.add, wts)
    out = functools.reduce(jnp.add, [w * o_s[p] for p, w in enumerate(wts)]) / total
    o_ref[...] = out.astype(BF16)


def _dilated_attn(q, k, v, *, batch, seq):
    n_tiles = seq // A_TILE
    heads, t, _ = q.shape
    blk = pl.BlockSpec((1, A_TILE, HEAD_DIM), lambda b, h, i: (h, b * n_tiles + i, 0))
    n16 = A_STEP * A_STEP
    return pl.pallas_call(
        _dilated_body,
        grid=(batch, heads, n_tiles),
        in_specs=[blk, blk, blk],
        out_specs=pl.BlockSpec((A_TILE, HEAD_DIM), lambda b, h, i: (b * n_tiles + i, h)),
        out_shape=jax.ShapeDtypeStruct((t, heads * HEAD_DIM), BF16),
        scratch_shapes=[
            pltpu.VMEM((A_TILE, HEAD_DIM), F32),
            pltpu.VMEM((A_TILE, HEAD_DIM), F32),
            pltpu.VMEM((A_STEP, A_PHASE4, HEAD_DIM), BF16),
            pltpu.VMEM((n16, A_BACK, HEAD_DIM), BF16),
        ] + [
            pltpu.VMEM((A_BACK + A_TILE, HEAD_DIM), BF16),
            pltpu.VMEM((A_STEP, A_BACK + A_PHASE4, HEAD_DIM), BF16),
            pltpu.VMEM((n16, 2 * A_BACK, HEAD_DIM), BF16),
        ] * 2 + [
            pltpu.VMEM((len(A_DILATIONS), A_TILE, HEAD_DIM), F32),
            pltpu.VMEM((len(A_DILATIONS), A_TILE, HEAD_DIM), F32),
            pltpu.VMEM((2, A_BACK, 2 * A_BACK), F32),
        ],
        compiler_params=_params("parallel", "parallel", "arbitrary"),
        name="dilated_attn",
    )(q, k, v)


def _mla_body(q_ref, k_ref, v_ref, o_ref, s_a, s_b, mx_a, mx_b, m_s, l_s, acc_s, *, tq, tk):
    qi = pl.program_id(2)
    n_tiles = tk // LANES
    n_blocks = 2 * qi + 2
    q = q_ref[0]
    m_s[...] = jnp.full(m_s.shape, MASK_VALUE, F32)
    l_s[...] = jnp.zeros(l_s.shape, F32)
    acc_s[...] = jnp.zeros(acc_s.shape, F32)

    def key_rows(n):
        return pl.ds(pl.multiple_of(n * tk, tk), tk)

    def logits(n, s_ref, mx_ref, diag_block):
        s = _dot_nt(q, k_ref[0, key_rows(n), :])
        tiles = [s[:, c * LANES:(c + 1) * LANES] for c in range(n_tiles)]
        if diag_block is not None:
            row = lax.broadcasted_iota(jnp.int32, (tq, LANES), 0)
            col = lax.broadcasted_iota(jnp.int32, (tq, LANES), 1) + diag_block * tk
            tiles = [jnp.where(col + c * LANES <= row, x, MASK_VALUE) for c, x in enumerate(tiles)]
        for c, x in enumerate(tiles):
            s_ref[:, c * LANES:(c + 1) * LANES] = x
        mx = jnp.max(functools.reduce(jnp.maximum, tiles), axis=-1, keepdims=True)
        mx_ref[...] = jnp.broadcast_to(mx, mx_ref.shape)

    def softmax_av(s_ref, mx_ref, n):
        m_prev = m_s[...]
        m_new = jnp.maximum(m_prev, mx_ref[...])
        alpha = jnp.exp2(m_prev - m_new)
        p = [jnp.exp2(s_ref[:, c * LANES:(c + 1) * LANES] - m_new) for c in range(n_tiles)]
        l_s[...] = alpha * l_s[...] + functools.reduce(jnp.add, p)
        pb = jnp.concatenate([x.astype(BF16) for x in p], axis=1)
        acc_s[...] = alpha * acc_s[...] + _dot(pb, v_ref[0, key_rows(n), :])
        m_s[...] = m_new

    @pl.when(qi > 0)
    def _():
        logits(0, s_a, mx_a, None)

    def pair(u, carry):
        logits(2 * u + 1, s_b, mx_b, None)
        softmax_av(s_a, mx_a, 2 * u)
        logits(2 * u + 2, s_a, mx_a, None)
        softmax_av(s_b, mx_b, 2 * u + 1)
        return carry

    lax.fori_loop(0, qi - 1, pair, 0)

    @pl.when(qi > 0)
    def _():
        logits(n_blocks - 3, s_b, mx_b, None)
        softmax_av(s_a, mx_a, n_blocks - 4)
        logits(n_blocks - 2, s_a, mx_a, 0)
        softmax_av(s_b, mx_b, n_blocks - 3)

    @pl.when(qi == 0)
    def _():
        logits(0, s_a, mx_a, 0)

    logits(n_blocks - 1, s_b, mx_b, 1)
    softmax_av(s_a, mx_a, n_blocks - 2)
    softmax_av(s_b, mx_b, n_blocks - 1)
    o_ref[...] = (acc_s[...] / jnp.sum(l_s[...], axis=-1, keepdims=True)).astype(BF16)


def _mla_attn(q, k, v, *, batch, seq, tq, tk):
    assert tq == 2 * tk
    heads, t, _ = q.shape
    nq = seq // tq
    kv = pl.BlockSpec((1, seq, LANES), lambda h, b, i: (h, b, 0))
    return pl.pallas_call(
        functools.partial(_mla_body, tq=tq, tk=tk),
        grid=(heads, batch, nq),
        in_specs=[pl.BlockSpec((1, tq, LANES), lambda h, b, i: (h, b * nq + i, 0)), kv, kv],
        out_specs=pl.BlockSpec((tq, LANES), lambda h, b, i: (b * nq + i, h)),
        out_shape=jax.ShapeDtypeStruct((t, heads * LANES), BF16),
        scratch_shapes=[pltpu.VMEM((tq, tk), F32)] * 2 + [pltpu.VMEM((tq, LANES), F32)] * 5,
        compiler_params=_params("parallel", "parallel", "arbitrary"),
        name="mla_attn",
    )(q, k, v)


def _odd_proj_body(h_ref, g_ref, win_ref, pw_ref, ps_ref, oc_ref, qd_ref, kd_ref, vd_ref, ext, *, tm, tiles_per_seq):
    i = pl.program_id(0)
    xn = _rms(h_ref[...], g_ref[...]).astype(BF16)
    proj = _dot(xn, win_ref[...])

    @pl.when(i % tiles_per_seq == 0)
    def _():
        ext[0:C_HALO, :] = jnp.zeros((C_HALO, C_WIDTH), F32)

    ext[C_HALO:C_HALO + tm, :] = proj[:, 0:C_WIDTH]
    t_in_seq = (i % tiles_per_seq) * tm + lax.broadcasted_iota(jnp.int32, (tm, 1), 0)
    for g, w in enumerate(C_WINDOWS):
        cols = slice(g * C_GROUP_DIM, (g + 1) * C_GROUP_DIM)
        s = ext[C_HALO - (w - 1):C_HALO + tm, cols]
        step = 1
        while step < w:
            s = s[step:] + s[:-step]
            step *= 2
        count = jnp.minimum(t_in_seq + 1, w).astype(F32)
        pooled = s / count - proj[:, cols]
        oc_ref[:, cols] = (_dot(pooled.astype(BF16), pw_ref[g]) * ps_ref[:, cols]).astype(BF16)
    ext[0:C_HALO, :] = ext[tm:tm + C_HALO, :]

    dw = D_HEADS * HEAD_DIM
    for hd in range(D_HEADS):
        lanes = lambda base: slice(base + hd * LANES, base + (hd + 1) * LANES)
        qd_ref[hd] = (proj[:, lanes(C_WIDTH)] * (HEAD_DIM ** -0.5 * LOG2_E)).astype(BF16)
        kd_ref[hd] = proj[:, lanes(C_WIDTH + dw)].astype(BF16)
        vd_ref[hd] = proj[:, lanes(C_WIDTH + 2 * dw)].astype(BF16)


def _odd_proj(h, g, w_in, pool_w, pool_scale, *, seq, tm):
    t, d = h.shape
    row = lambda w: pl.BlockSpec((tm, w), lambda i: (i, 0))
    heads = pl.BlockSpec((D_HEADS, tm, LANES), lambda i: (0, i, 0))
    shp = jax.ShapeDtypeStruct((D_HEADS, t, LANES), BF16)
    return pl.pallas_call(
        functools.partial(_odd_proj_body, tm=tm, tiles_per_seq=seq // tm),
        grid=(t // tm,),
        in_specs=[row(d), _resident((1, d)), _resident(w_in.shape), _resident(pool_w.shape), _resident(pool_scale.shape)],
        out_specs=[row(C_WIDTH), heads, heads, heads],
        out_shape=[jax.ShapeDtypeStruct((t, C_WIDTH), BF16), shp, shp, shp],
        scratch_shapes=[pltpu.VMEM((C_HALO + tm, C_WIDTH), F32)],
        compiler_params=_params("arbitrary"),
        name="odd_proj",
    )(h, g, w_in, pool_w, pool_scale)


SB_SUB = 256
SB_DEAD_CARRY = 160.0


def _sb_body(q_ref, k_ref, v_ref, o_ref, z_a, z_b, e_a, e_b, carry_s, acc_s, *, tq, tk):
    qi = pl.program_id(2)
    n_blocks = 2 * qi + 2
    q = q_ref[0]
    carry_s[...] = jnp.zeros(carry_s.shape, F32)
    acc_s[...] = jnp.zeros(acc_s.shape, F32)
    later = (lax.broadcasted_iota(jnp.int32, (SB_SUB, SB_SUB), 0) > lax.broadcasted_iota(jnp.int32, (SB_SUB, SB_SUB), 1))
    later = jnp.where(later, 1.0, 0.0).astype(BF16)

    def key_rows(n):
        kb = jnp.maximum(n_blocks - 1 - n, 0)
        return pl.ds(pl.multiple_of(kb * tk, tk), tk)

    def logits(n, z_ref):
        z_ref[...] = _dot_nt(q, k_ref[0, key_rows(n), :])

    def gate(z_ref, e_ref, diag_block):
        carry = carry_s[...]
        for sub in reversed(range(tk // SB_SUB)):
            z = z_ref[:, sub * SB_SUB:(sub + 1) * SB_SUB]
            softplus = jnp.maximum(z, 0.0) + jnp.log2(1.0 + jnp.exp2(-jnp.abs(z)))
            arg = z - softplus
            if diag_block is None:
                keep = softplus
            else:
                row = lax.broadcasted_iota(jnp.int32, (tq, SB_SUB), 0)
                col = lax.broadcasted_iota(jnp.int32, (tq, SB_SUB), 1) + diag_block * tk + sub * SB_SUB
                keep = jnp.where(col < row, softplus, 0.0)
                arg = jnp.where(col < row, arg, MASK_VALUE)
            expo = arg - _dot(keep.astype(BF16), later)
            for c in range(SB_SUB // LANES):
                lo = sub * SB_SUB + c * LANES
                e_ref[:, lo:lo + LANES] = expo[:, c * LANES:(c + 1) * LANES] - carry
            carry = carry + jnp.sum(keep, axis=-1, keepdims=True)
        carry_s[...] = carry

    def weigh(e_ref, n):
        acc_s[...] += _dot(jnp.exp2(e_ref[...]).astype(BF16), v_ref[0, key_rows(n), :])

    logits(0, z_a)
    logits(1, z_b)
    gate(z_a, e_a, 1)
    logits(2, z_a)
    gate(z_b, e_b, 0)
    weigh(e_a, 0)

    def pair(state):
        u, _ = state
        n = 2 * u + 2
        logits(n + 1, z_b)
        gate(z_a, e_a, None)
        weigh(e_b, n - 1)
        logits(n + 2, z_a)
        gate(z_b, e_b, None)
        weigh(e_a, n)
        return u + 1, jnp.min(carry_s[...])

    def more_blocks(state):
        u, min_carry = state
        return (u < qi) & (min_carry < SB_DEAD_CARRY)

    pairs_done, _ = lax.while_loop(more_blocks, pair, (jnp.int32(0), jnp.min(carry_s[...])))
    weigh(e_b, 2 * pairs_done + 1)
    o_ref[...] = acc_s[...].astype(BF16)


def _sb_attn(q, k, v, *, batch, seq, tq, tk):
    assert tq == 2 * tk and tk % SB_SUB == 0
    heads, t, _ = q.shape
    nq = seq // tq
    kv = pl.BlockSpec((1, seq, LANES), lambda h, b, i: (h, b, 0))
    return pl.pallas_call(
        functools.partial(_sb_body, tq=tq, tk=tk),
        grid=(heads, batch, nq),
        in_specs=[pl.BlockSpec((1, tq, LANES), lambda h, b, i: (h, b * nq + i, 0)), kv, kv],
        out_specs=pl.BlockSpec((tq, LANES), lambda h, b, i: (b * nq + i, h)),
        out_shape=jax.ShapeDtypeStruct((t, heads * LANES), BF16),
        scratch_shapes=[pltpu.VMEM((tq, tk), F32)] * 4 + [pltpu.VMEM((tq, LANES), F32)] * 2,
        compiler_params=_params("parallel", "parallel", "arbitrary"),
        name="sb_attn",
    )(q, k, v)


def _latent_head_lanes():
    half_pe, half_nope = B_ROPE // 2, B_NOPE // 2
    pe = np.concatenate([np.arange(half_pe), LANES // 2 + np.arange(half_pe)])
    nope = np.concatenate([half_pe + np.arange(half_nope), LANES // 2 + half_pe + np.arange(half_nope)])
    return pe, nope


def _scatter_cols(w, src_cols, dst_cols, width):
    out = jnp.zeros((w.shape[0], width), w.dtype)
    return out.at[:, np.asarray(dst_cols)].set(w[:, np.asarray(src_cols)])


def _even_weights(w_in, w_q_up, w_kv_up):
    pe, nope = _latent_head_lanes()
    a_w = A_HEADS * HEAD_DIM
    o_kr = 3 * a_w + B_Q_LORA + B_KV_LORA
    k_rope = _scatter_cols(w_in, o_kr + np.arange(B_ROPE), pe, LANES)
    w_in_p = jnp.concatenate([w_in[:, :o_kr], k_rope], axis=1).astype(BF16)
    qd, kvd = B_NOPE + B_ROPE, B_NOPE + B_V
    src_q, dst_q, src_k, dst_k, src_v = [], [], [], [], []
    for hd in range(B_HEADS):
        src_q += list(hd * qd + np.arange(B_NOPE)) + list(hd * qd + B_NOPE + np.arange(B_ROPE))
        dst_q += list(hd * LANES + nope) + list(hd * LANES + pe)
        src_k += list(hd * kvd + np.arange(B_NOPE))
        dst_k += list(hd * LANES + nope)
        src_v += list(hd * kvd + B_NOPE + np.arange(B_V))
    w_q = _scatter_cols(w_q_up, src_q, dst_q, B_HEADS * LANES).astype(BF16)
    w_k = _scatter_cols(w_kv_up, src_k, dst_k, B_HEADS * LANES)
    w_kv = jnp.concatenate([w_k, w_kv_up[:, np.asarray(src_v)]], axis=1).astype(BF16)
    return w_in_p, w_q, w_kv


def _rope_frequencies():
    inv_a = ROPE_THETA ** (-jnp.arange(0, HEAD_DIM, 2, dtype=F32) / HEAD_DIM)
    inv_b = ROPE_THETA ** (-jnp.arange(0, B_ROPE, 2, dtype=F32) / B_ROPE)
    pe, _ = _latent_head_lanes()
    inv_b_lanes = jnp.zeros((LANES,), F32).at[pe].set(jnp.concatenate([inv_b, inv_b]))
    return jnp.concatenate([inv_a, inv_a])[None, :], inv_b_lanes[None, :]


def kernel(x, positions, norm_g, ffn_w_gate, ffn_w_up, ffn_w_down, even_w_in, even_q_norm, even_w_q_up, even_kv_norm, even_w_kv_up, even_w_out, odd_w_in, odd_pool_w, odd_pool_scale, odd_w_out, final_norm):
    batch, seq, d = x.shape
    t = batch * seq
    assert seq % A_TILE == 0
    tm = 256
    h = x.reshape(t, d).astype(F32)
    pos = positions.reshape(t, 1).astype(jnp.int32)
    bf = lambda w: w.astype(BF16)
    gain = lambda i, j: norm_g[i, j][None, :].astype(F32)
    ffn_w = lambda i, j: (bf(ffn_w_gate[i, j]), bf(ffn_w_up[i, j]), bf(ffn_w_down[i, j]))

    h = _ffn(h, [], gain(0, 0), *ffn_w(0, 0), tm=tm)
    tabs = _rope_tables(pos, *_rope_frequencies(), tm=1024)
    w_in_p, w_q, w_kv = _even_weights(even_w_in[0], even_w_q_up[0], even_w_kv_up[0])
    qa, ka, va, qb, kb, vb = _even_proj(h, gain(0, 1), w_in_p, tabs, even_q_norm[0][None, :].astype(F32), w_q,
                                        even_kv_norm[0][None, :].astype(F32), w_kv, tm=tm)
    out_a = _dilated_attn(qa, ka, va, batch=batch, seq=seq)
    out_b = _mla_attn(qb, kb, vb, batch=batch, seq=seq, tq=1024, tk=512)
    a_w = A_HEADS * HEAD_DIM
    w_out = bf(even_w_out[0])
    h = _ffn(h, [(out_a, w_out[:a_w]), (out_b, w_out[a_w:])], gain(0, 2), *ffn_w(0, 1), tm=tm)

    h = _ffn(h, [], gain(1, 0), *ffn_w(1, 0), tm=tm)
    out_c, qd, kd, vd = _odd_proj(h, gain(1, 1), bf(odd_w_in[0]), bf(odd_pool_w[0]),
                                  odd_pool_scale[0][None, :].astype(F32), seq=seq, tm=tm)
    out_d = _sb_attn(qd, kd, vd, batch=batch, seq=seq, tq=512, tk=256)
    w_out = bf(odd_w_out[0])
    h = _ffn(h, [(out_c, w_out[:C_WIDTH]), (out_d, w_out[C_WIDTH:])], gain(1, 2), *ffn_w(1, 1),
             final_norm[None, :].astype(F32), tm=tm)
    return h.reshape(batch, seq, d).astype(x.dtype)
```

```python
import functools

import jax
import jax.numpy as jnp
import numpy as np
from jax import lax
from jax.experimental import pallas as pl
from jax.experimental.pallas import tpu as pltpu

F32 = jnp.float32
BF16 = jnp.bfloat16

HEAD_DIM = 128
LANES = 128
ROPE_THETA = 10000.0
NORM_EPS = 1e-6
A_HEADS = 6
A_DILATIONS = (1, 4, 16)
A_BACK = 128
B_HEADS = 4
B_Q_LORA = 384
B_KV_LORA = 256
B_NOPE = 64
B_ROPE = 32
B_V = 128
C_WINDOWS = (2, 4, 8, 16)
C_GROUP_DIM = 128
C_WIDTH = len(C_WINDOWS) * C_GROUP_DIM
C_HALO = 16
D_HEADS = 4
MASK_VALUE = -1e30
LOG2_E = 1.4426950408889634
VMEM_LIMIT_BYTES = 56 * 1024 * 1024


def _params(*semantics):
    return pltpu.CompilerParams(dimension_semantics=semantics, vmem_limit_bytes=VMEM_LIMIT_BYTES)


def _resident(shape):
    return pl.BlockSpec(shape, lambda *_: (0,) * len(shape), pipeline_mode=pl.Buffered(1))


def _rms(x, g):
    ms = jnp.mean(x * x, axis=-1, keepdims=True)
    return x * lax.rsqrt(ms + NORM_EPS) * g


def _dot(a, b):
    return jnp.dot(a, b, preferred_element_type=F32)


def _dot_nt(a, b):
    return lax.dot_general(a, b, (((1,), (1,)), ((), ())), preferred_element_type=F32)


def _ffn_body(*refs, mix_widths, final_norm):
    h_ref = refs[0]
    n_mix = len(mix_widths)
    mix_refs = refs[1:1 + n_mix]
    base = 1 + n_mix
    wo_ref = refs[base] if n_mix else None
    base += 1 if n_mix else 0
    g_ref, wg_ref, wu_ref, wd_ref = refs[base:base + 4]
    fg_ref = refs[base + 4] if final_norm else None
    o_ref = refs[-1]

    h = h_ref[...]
    row = 0
    for m_ref, width in zip(mix_refs, mix_widths):
        h = h + _dot(m_ref[...], wo_ref[row:row + width, :])
        row += width
    xn = _rms(h, g_ref[...]).astype(BF16)
    gate = _dot(xn, wg_ref[...])
    up = _dot(xn, wu_ref[...])
    act = (gate * jax.nn.sigmoid(gate) * up).astype(BF16)
    y = h + 0.5 * _dot(act, wd_ref[...])
    if final_norm:
        y = _rms(y, fg_ref[...])
    o_ref[...] = y


def _ffn(h, mix, w_out, g, wg, wu, wd, which, final_g=None, *, tm):
    t, d = h.shape
    d_ff = wg.shape[-1]
    row = lambda w: pl.BlockSpec((tm, w), lambda i: (i, 0))
    pick = lambda r, c: pl.BlockSpec((None, None, r, c), lambda i: (*which, 0, 0), pipeline_mode=pl.Buffered(1))
    args, specs = [h], [row(d)]
    for m in mix:
        args.append(m)
        specs.append(row(m.shape[1]))
    if mix:
        args.append(w_out)
        specs.append(_resident(w_out.shape))
    args += [g, wg, wu, wd]
    specs += [_resident((1, d)), pick(d, d_ff), pick(d, d_ff), pick(d_ff, d)]
    if final_g is not None:
        args.append(final_g)
        specs.append(_resident((1, d)))
    return pl.pallas_call(
        functools.partial(_ffn_body, mix_widths=tuple(m.shape[1] for m in mix), final_norm=final_g is not None),
        grid=(t // tm,),
        in_specs=specs,
        out_specs=row(d),
        out_shape=jax.ShapeDtypeStruct((t, d), F32),
        compiler_params=_params("parallel"),
        name="ffn",
    )(*args)


def _rope_tables_body(pos_ref, inv_ref, ca_ref, sa_ref, cb_ref, sb_ref):
    pos = pos_ref[...].astype(F32)
    lane = lax.broadcasted_iota(jnp.int32, (1, LANES), 1)
    half = LANES // 2
    sign = jnp.where(lane < half, -1.0, 1.0).astype(F32)
    ang = pos * inv_ref[...]
    c, s = jnp.cos(ang), jnp.sin(ang)
    c_sw, s_sw = pltpu.roll(c, half, 1), pltpu.roll(s, half, 1)
    ca_ref[...] = jnp.where(lane < half, c, c_sw)
    sa_ref[...] = sign * jnp.where(lane < half, s, s_sw)
    first, second = lane < B_ROPE // 2, (lane >= half) & (lane < half + B_ROPE // 2)
    cb_ref[...] = jnp.where(first, c_sw, jnp.where(second, c, 1.0))
    sb_ref[...] = sign * jnp.where(first, s_sw, jnp.where(second, s, 0.0))


def _rope_tables(pos, inv, *, tm):
    t = pos.shape[0]
    tab = pl.BlockSpec((tm, LANES), lambda i: (i, 0))
    return pl.pallas_call(
        _rope_tables_body,
        grid=(t // tm,),
        in_specs=[pl.BlockSpec((tm, 1), lambda i: (i, 0)), _resident((1, LANES))],
        out_specs=[tab] * 4,
        out_shape=[jax.ShapeDtypeStruct((t, LANES), F32)] * 4,
        compiler_params=_params("parallel"),
        name="rope_tables",
    )(pos, inv)


def _rotate(x, c, s):
    return x * c + pltpu.roll(x, LANES // 2, 1) * s


def _even_proj_body(h_ref, g_ref, win_ref, ca_ref, sa_ref, cb_ref, sb_ref, qn_ref, wq_ref, kvn_ref, wkv_ref,
                    qa_ref, ka_ref, va_ref, qb_ref, kb_ref, vb_ref):
    xn = _rms(h_ref[...], g_ref[...]).astype(BF16)
    proj = _dot(xn, win_ref[...])
    ca, sa, cb, sb = ca_ref[...], sa_ref[...], cb_ref[...], sb_ref[...]
    a_w = A_HEADS * HEAD_DIM
    blk = lambda base, i: slice(base + i * LANES, base + (i + 1) * LANES)
    for hd in range(A_HEADS):
        qa_ref[hd] = (_rotate(proj[:, blk(0, hd)], ca, sa) * (HEAD_DIM ** -0.5 * LOG2_E)).astype(BF16)
        ka_ref[hd] = _rotate(proj[:, blk(a_w, hd)], ca, sa).astype(BF16)
        va_ref[hd] = proj[:, blk(2 * a_w, hd)].astype(BF16)
    o_cq = 3 * a_w
    o_ckv = o_cq + B_Q_LORA
    o_kr = o_ckv + B_KV_LORA
    cq = _rms(proj[:, o_cq:o_ckv], qn_ref[...]).astype(BF16)
    ckv = _rms(proj[:, o_ckv:o_kr], kvn_ref[...]).astype(BF16)
    qb = _dot(cq, wq_ref[...])
    kv = _dot(ckv, wkv_ref[...])
    k_pe = _rotate(proj[:, o_kr:o_kr + LANES], cb, sb)
    q_scale = (B_NOPE + B_ROPE) ** -0.5 * LOG2_E
    for hd in range(B_HEADS):
        qb_ref[hd] = (_rotate(qb[:, blk(0, hd)], cb, sb) * q_scale).astype(BF16)
        kb_ref[hd] = (kv[:, blk(0, hd)] + k_pe).astype(BF16)
        vb_ref[hd] = kv[:, blk(B_HEADS * LANES, hd)].astype(BF16)


def _even_proj(h, g, w_in, tabs, q_norm, w_q, kv_norm, w_kv, *, tm):
    t, d = h.shape
    row = lambda w: pl.BlockSpec((tm, w), lambda i: (i, 0))
    heads = lambda n: pl.BlockSpec((n, tm, LANES), lambda i: (0, i, 0))
    shp = lambda n: jax.ShapeDtypeStruct((n, t, LANES), BF16)
    return pl.pallas_call(
        _even_proj_body,
        grid=(t // tm,),
        in_specs=[row(d), _resident((1, d)), _resident(w_in.shape)] + [row(LANES)] * 4
        + [_resident(q_norm.shape), _resident(w_q.shape), _resident(kv_norm.shape), _resident(w_kv.shape)],
        out_specs=[heads(A_HEADS)] * 3 + [heads(B_HEADS)] * 3,
        out_shape=[shp(A_HEADS)] * 3 + [shp(B_HEADS)] * 3,
        compiler_params=_params("parallel"),
        name="even_proj",
    )(h, g, w_in, *tabs, q_norm, w_q, kv_norm, w_kv)


A_TILE = A_BACK * max(A_DILATIONS)
A_STEP = 4
assert A_DILATIONS == (1, A_STEP, A_STEP * A_STEP)
A_PHASE4 = A_TILE // A_STEP


def _dilated_body(q_ref, k_ref, v_ref, o_ref, nat, p4f, q4, q16, k1, k4, k16, v1, v4, v16, o_s, lse_s, bias):
    i = pl.program_id(2)

    @pl.when(i > 0)
    def _():
        for buf in (k1, v1):
            buf[0:A_BACK] = buf[A_TILE:A_TILE + A_BACK]
        for buf in (k4, v4):
            buf[:, 0:A_BACK] = buf[:, A_PHASE4:A_PHASE4 + A_BACK]
        for buf in (k16, v16):
            buf[:, 0:A_BACK] = buf[:, A_BACK:2 * A_BACK]

    @pl.when(i == 0)
    def _():
        for buf in (k1, v1):
            buf[0:A_BACK] = jnp.zeros((A_BACK, HEAD_DIM), BF16)
        for buf in (k4, v4, k16, v16):
            buf[:, 0:A_BACK] = jnp.zeros((buf.shape[0], A_BACK, HEAD_DIM), BF16)

    def regroup(x_ref, d4, d16, halo):
        nat[...] = x_ref[0].astype(F32)
        for r4 in range(A_STEP):
            rows = nat[pl.ds(r4, A_PHASE4, stride=A_STEP), :]
            p4f[r4 * A_PHASE4:(r4 + 1) * A_PHASE4] = rows
            d4[r4, halo:halo + A_PHASE4] = rows.astype(BF16)
        for r16 in range(A_STEP * A_STEP):
            r4, sub = r16 % A_STEP, r16 // A_STEP
            rows = p4f[pl.ds(r4 * A_PHASE4 + sub, A_BACK, stride=A_STEP), :]
            d16[r16, halo:halo + A_BACK] = rows.astype(BF16)

    regroup(q_ref, q4, q16, 0)
    k1[A_BACK:A_BACK + A_TILE] = k_ref[0]
    regroup(k_ref, k4, k16, A_BACK)
    v1[A_BACK:A_BACK + A_TILE] = v_ref[0]
    regroup(v_ref, v4, v16, A_BACK)

    qi = lax.broadcasted_iota(jnp.int32, (A_BACK, 2 * A_BACK), 0)
    ki = lax.broadcasted_iota(jnp.int32, (A_BACK, 2 * A_BACK), 1)
    band = (ki >= qi) & (ki <= qi + A_BACK)
    bias[0] = jnp.where(band, 0.0, MASK_VALUE)
    bias[1] = jnp.where(band & (ki >= jnp.where(i > 0, 0, A_BACK)), 0.0, MASK_VALUE)

    def band_block(p, qb, kb, vb, first, out_rows):
        s = _dot_nt(qb, kb) + bias[1 if first else 0]
        m = jnp.max(s, axis=-1, keepdims=True)
        e = jnp.exp2(s - m)
        den = jnp.sum(e, axis=-1, keepdims=True)
        o_s[p, out_rows, :] = _dot(e.astype(BF16), vb) / den
        lse_s[p, out_rows, :] = jnp.broadcast_to(m + jnp.log2(den), (A_BACK, HEAD_DIM))

    for j in range(A_TILE // A_BACK):
        blk, bnd = slice(j * A_BACK, (j + 1) * A_BACK), slice(j * A_BACK, (j + 2) * A_BACK)
        band_block(0, q_ref[0, blk, :], k1[bnd], v1[bnd], j == 0, blk)
    for r4 in range(A_STEP):
        for j in range(A_PHASE4 // A_BACK):
            blk, bnd = slice(j * A_BACK, (j + 1) * A_BACK), slice(j * A_BACK, (j + 2) * A_BACK)
            band_block(1, q4[r4, blk], k4[r4, bnd], v4[r4, bnd], j == 0,
                       pl.ds(r4 + j * A_BACK * A_STEP, A_BACK, stride=A_STEP))
    for r16 in range(A_STEP * A_STEP):
        band_block(2, q16[r16], k16[r16], v16[r16], True, pl.ds(r16, A_BACK, stride=A_STEP * A_STEP))

    n_pat = len(A_DILATIONS)
    lse = [lse_s[p] for p in range(n_pat)]
    top = functools.reduce(jnp.maximum, lse)
    wts = [jnp.exp2(x - top) for x in lse]
    total = functools.reduce(jnp.add, wts)
    out = functools.reduce(jnp.add, [w * o_s[p] for p, w in enumerate(wts)]) / total
    o_ref[...] = out.astype(BF16)


def _dilated_attn(q, k, v, *, batch, seq):
    n_tiles = seq // A_TILE
    heads, t, _ = q.shape
    blk = pl.BlockSpec((1, A_TILE, HEAD_DIM), lambda b, h, i: (h, b * n_tiles + i, 0))
    n16 = A_STEP * A_STEP
    return pl.pallas_call(
        _dilated_body,
        grid=(batch, heads, n_tiles),
        in_specs=[blk, blk, blk],
        out_specs=pl.BlockSpec((A_TILE, HEAD_DIM), lambda b, h, i: (b * n_tiles + i, h)),
        out_shape=jax.ShapeDtypeStruct((t, heads * HEAD_DIM), BF16),
        scratch_shapes=[
            pltpu.VMEM((A_TILE, HEAD_DIM), F32),
            pltpu.VMEM((A_TILE, HEAD_DIM), F32),
            pltpu.VMEM((A_STEP, A_PHASE4, HEAD_DIM), BF16),
            pltpu.VMEM((n16, A_BACK, HEAD_DIM), BF16),
        ] + [
            pltpu.VMEM((A_BACK + A_TILE, HEAD_DIM), BF16),
            pltpu.VMEM((A_STEP, A_BACK + A_PHASE4, HEAD_DIM), BF16),
            pltpu.VMEM((n16, 2 * A_BACK, HEAD_DIM), BF16),
        ] * 2 + [
            pltpu.VMEM((len(A_DILATIONS), A_TILE, HEAD_DIM), F32),
            pltpu.VMEM((len(A_DILATIONS), A_TILE, HEAD_DIM), F32),
            pltpu.VMEM((2, A_BACK, 2 * A_BACK), F32),
        ],
        compiler_params=_params("parallel", "parallel", "arbitrary"),
        name="dilated_attn",
    )(q, k, v)


def _mla_body(q_ref, k_ref, v_ref, o_ref, s_a, s_b, mx_a, mx_b, m_s, l_s, acc_s, *, tq, tk):
    qi = pl.program_id(2)
    n_tiles = tk // LANES
    n_blocks = 2 * qi + 2
    m_s[...] = jnp.full(m_s.shape, MASK_VALUE, F32)
    l_s[...] = jnp.zeros(l_s.shape, F32)
    acc_s[...] = jnp.zeros(acc_s.shape, F32)

    def key_rows(n):
        return pl.ds(pl.multiple_of(n * tk, tk), tk)

    def logits(n, s_ref, mx_ref, diag_block):
        s = _dot_nt(q_ref[0], k_ref[0, key_rows(n), :])
        tiles = [s[:, c * LANES:(c + 1) * LANES] for c in range(n_tiles)]
        if diag_block is not None:
            row = lax.broadcasted_iota(jnp.int32, (tq, LANES), 0)
            col = lax.broadcasted_iota(jnp.int32, (tq, LANES), 1) + diag_block * tk
            tiles = [jnp.where(col + c * LANES <= row, x, MASK_VALUE) for c, x in enumerate(tiles)]
        for c, x in enumerate(tiles):
            s_ref[:, c * LANES:(c + 1) * LANES] = x
        mx = jnp.max(functools.reduce(jnp.maximum, tiles), axis=-1, keepdims=True)
        mx_ref[...] = jnp.broadcast_to(mx, mx_ref.shape)

    def softmax_av(s_ref, mx_ref, n):
        m_prev = m_s[...]
        m_new = jnp.maximum(m_prev, mx_ref[...])
        alpha = jnp.exp2(m_prev - m_new)
        p = [jnp.exp2(s_ref[:, c * LANES:(c + 1) * LANES] - m_new) for c in range(n_tiles)]
        l_s[...] = alpha * l_s[...] + functools.reduce(jnp.add, p)
        pb = jnp.concatenate([x.astype(BF16) for x in p], axis=1)
        acc_s[...] = alpha * acc_s[...] + _dot(pb, v_ref[0, key_rows(n), :])
        m_s[...] = m_new

    @pl.when(qi > 0)
    def _():
        logits(0, s_a, mx_a, None)

    def pair(u, carry):
        logits(2 * u + 1, s_b, mx_b, None)
        softmax_av(s_a, mx_a, 2 * u)
        logits(2 * u + 2, s_a, mx_a, None)
        softmax_av(s_b, mx_b, 2 * u + 1)
        return carry

    lax.fori_loop(0, qi - 1, pair, 0)

    @pl.when(qi > 0)
    def _():
        logits(n_blocks - 3, s_b, mx_b, None)
        softmax_av(s_a, mx_a, n_blocks - 4)
        logits(n_blocks - 2, s_a, mx_a, 0)
        softmax_av(s_b, mx_b, n_blocks - 3)

    @pl.when(qi == 0)
    def _():
        logits(0, s_a, mx_a, 0)

    logits(n_blocks - 1, s_b, mx_b, 1)
    softmax_av(s_a, mx_a, n_blocks - 2)
    softmax_av(s_b, mx_b, n_blocks - 1)
    o_ref[...] = (acc_s[...] / jnp.sum(l_s[...], axis=-1, keepdims=True)).astype(BF16)


def _mla_attn(q, k, v, *, batch, seq, tq, tk):
    assert tq == 2 * tk
    heads, t, _ = q.shape
    nq = seq // tq
    kv = pl.BlockSpec((1, seq, LANES), lambda h, b, i: (h, b, 0))
    return pl.pallas_call(
        functools.partial(_mla_body, tq=tq, tk=tk),
        grid=(heads, batch, nq),
        in_specs=[pl.BlockSpec((1, tq, LANES), lambda h, b, i: (h, b * nq + i, 0)), kv, kv],
        out_specs=pl.BlockSpec((tq, LANES), lambda h, b, i: (b * nq + i, h)),
        out_shape=jax.ShapeDtypeStruct((t, heads * LANES), BF16),
        scratch_shapes=[pltpu.VMEM((tq, tk), F32)] * 2 + [pltpu.VMEM((tq, LANES), F32)] * 5,
        compiler_params=_params("parallel", "parallel", "arbitrary"),
        name="mla_attn",
    )(q, k, v)


def _odd_proj_body(h_ref, g_ref, win_ref, pw_ref, ps_ref, oc_ref, qd_ref, kd_ref, vd_ref, ext, *, tm, tiles_per_seq):
    i = pl.program_id(0)
    xn = _rms(h_ref[...], g_ref[...]).astype(BF16)
    proj = _dot(xn, win_ref[...])

    @pl.when(i % tiles_per_seq == 0)
    def _():
        ext[0:C_HALO, :] = jnp.zeros((C_HALO, C_WIDTH), F32)

    ext[C_HALO:C_HALO + tm, :] = proj[:, 0:C_WIDTH]
    t_in_seq = (i % tiles_per_seq) * tm + lax.broadcasted_iota(jnp.int32, (tm, 1), 0)
    for g, w in enumerate(C_WINDOWS):
        cols = slice(g * C_GROUP_DIM, (g + 1) * C_GROUP_DIM)
        s = ext[C_HALO - (w - 1):C_HALO + tm, cols]
        step = 1
        while step < w:
            s = s[step:] + s[:-step]
            step *= 2
        count = jnp.minimum(t_in_seq + 1, w).astype(F32)
        pooled = s / count - proj[:, cols]
        oc_ref[:, cols] = (_dot(pooled.astype(BF16), pw_ref[g]) * ps_ref[:, cols]).astype(BF16)
    ext[0:C_HALO, :] = ext[tm:tm + C_HALO, :]

    dw = D_HEADS * HEAD_DIM
    for hd in range(D_HEADS):
        lanes = lambda base: slice(base + hd * LANES, base + (hd + 1) * LANES)
        qd_ref[hd] = (proj[:, lanes(C_WIDTH)] * (HEAD_DIM ** -0.5 * LOG2_E)).astype(BF16)
        kd_ref[hd] = proj[:, lanes(C_WIDTH + dw)].astype(BF16)
        vd_ref[hd] = proj[:, lanes(C_WIDTH + 2 * dw)].astype(BF16)


def _odd_proj(h, g, w_in, pool_w, pool_scale, *, seq, tm):
    t, d = h.shape
    row = lambda w: pl.BlockSpec((tm, w), lambda i: (i, 0))
    heads = pl.BlockSpec((D_HEADS, tm, LANES), lambda i: (0, i, 0))
    shp = jax.ShapeDtypeStruct((D_HEADS, t, LANES), BF16)
    return pl.pallas_call(
        functools.partial(_odd_proj_body, tm=tm, tiles_per_seq=seq // tm),
        grid=(t // tm,),
        in_specs=[row(d), _resident((1, d)), _resident(w_in.shape), _resident(pool_w.shape), _resident(pool_scale.shape)],
        out_specs=[row(C_WIDTH), heads, heads, heads],
        out_shape=[jax.ShapeDtypeStruct((t, C_WIDTH), BF16), shp, shp, shp],
        scratch_shapes=[pltpu.VMEM((C_HALO + tm, C_WIDTH), F32)],
        compiler_params=_params("arbitrary"),
        name="odd_proj",
    )(h, g, w_in, pool_w, pool_scale)


SB_SUB = 256
SB_DEAD_CARRY = 160.0


def _sb_body(q_ref, k_ref, v_ref, o_ref, z_a, z_b, e_a, e_b, carry_s, acc_s, *, tq, tk):
    qi = pl.program_id(2)
    n_blocks = 2 * qi + 2
    q = q_ref[0]
    carry_s[...] = jnp.zeros(carry_s.shape, F32)
    acc_s[...] = jnp.zeros(acc_s.shape, F32)
    later = (lax.broadcasted_iota(jnp.int32, (SB_SUB, SB_SUB), 0) > lax.broadcasted_iota(jnp.int32, (SB_SUB, SB_SUB), 1))
    later = jnp.where(later, 1.0, 0.0).astype(BF16)

    def key_rows(n):
        kb = jnp.maximum(n_blocks - 1 - n, 0)
        return pl.ds(pl.multiple_of(kb * tk, tk), tk)

    def logits(n, z_ref):
        z_ref[...] = _dot_nt(q, k_ref[0, key_rows(n), :])

    def gate(z_ref, e_ref, diag_block):
        carry = carry_s[...]
        for sub in reversed(range(tk // SB_SUB)):
            z = z_ref[:, sub * SB_SUB:(sub + 1) * SB_SUB]
            softplus = jnp.maximum(z, 0.0) + jnp.log2(1.0 + jnp.exp2(-jnp.abs(z)))
            arg = z - softplus
            if diag_block is None:
                keep = softplus
            else:
                row = lax.broadcasted_iota(jnp.int32, (tq, SB_SUB), 0)
                col = lax.broadcasted_iota(jnp.int32, (tq, SB_SUB), 1) + diag_block * tk + sub * SB_SUB
                keep = jnp.where(col < row, softplus, 0.0)
                arg = jnp.where(col < row, arg, MASK_VALUE)
            expo = arg - _dot(keep.astype(BF16), later)
            for c in range(SB_SUB // LANES):
                lo = sub * SB_SUB + c * LANES
                e_ref[:, lo:lo + LANES] = expo[:, c * LANES:(c + 1) * LANES] - carry
            carry = carry + jnp.sum(keep, axis=-1, keepdims=True)
        carry_s[...] = carry

    def weigh(e_ref, n):
        acc_s[...] += _dot(jnp.exp2(e_ref[...]).astype(BF16), v_ref[0, key_rows(n), :])

    logits(0, z_a)
    logits(1, z_b)
    gate(z_a, e_a, 1)
    logits(2, z_a)
    gate(z_b, e_b, 0)
    weigh(e_a, 0)

    def pair(state):
        u, _ = state
        n = 2 * u + 2
        logits(n + 1, z_b)
        gate(z_a, e_a, None)
        weigh(e_b, n - 1)
        logits(n + 2, z_a)
        gate(z_b, e_b, None)
        weigh(e_a, n)
        return u + 1, jnp.min(carry_s[...])

    def more_blocks(state):
        u, min_carry = state
        return (u < qi) & (min_carry < SB_DEAD_CARRY)

    pairs_done, _ = lax.while_loop(more_blocks, pair, (jnp.int32(0), jnp.min(carry_s[...])))
    weigh(e_b, 2 * pairs_done + 1)
    o_ref[...] = acc_s[...].astype(BF16)


def _sb_attn(q, k, v, *, batch, seq, tq, tk):
    assert tq == 2 * tk and tk % SB_SUB == 0
    heads, t, _ = q.shape
    nq = seq // tq
    kv = pl.BlockSpec((1, seq, LANES), lambda h, b, i: (h, b, 0))
    return pl.pallas_call(
        functools.partial(_sb_body, tq=tq, tk=tk),
        grid=(heads, batch, nq),
        in_specs=[pl.BlockSpec((1, tq, LANES), lambda h, b, i: (h, b * nq + i, 0)), kv, kv],
        out_specs=pl.BlockSpec((tq, LANES), lambda h, b, i: (b * nq + i, h)),
        out_shape=jax.ShapeDtypeStruct((t, heads * LANES), BF16),
        scratch_shapes=[pltpu.VMEM((tq, tk), F32)] * 4 + [pltpu.VMEM((tq, LANES), F32)] * 2,
        compiler_params=_params("parallel", "parallel", "arbitrary"),
        name="sb_attn",
    )(q, k, v)


def _latent_head_lanes():
    half_pe, half_nope = B_ROPE // 2, B_NOPE // 2
    pe = np.concatenate([np.arange(half_pe), LANES // 2 + np.arange(half_pe)])
    nope = np.concatenate([half_pe + np.arange(half_nope), LANES // 2 + half_pe + np.arange(half_nope)])
    return pe, nope


def _scatter_cols(w, src_cols, dst_cols, width):
    out = jnp.zeros((w.shape[0], width), w.dtype)
    return out.at[:, np.asarray(dst_cols)].set(w[:, np.asarray(src_cols)])


def _even_weights(w_in, w_q_up, w_kv_up):
    pe, nope = _latent_head_lanes()
    a_w = A_HEADS * HEAD_DIM
    o_kr = 3 * a_w + B_Q_LORA + B_KV_LORA
    k_rope = _scatter_cols(w_in, o_kr + np.arange(B_ROPE), pe, LANES)
    w_in_p = jnp.concatenate([w_in[:, :o_kr], k_rope], axis=1).astype(BF16)
    qd, kvd = B_NOPE + B_ROPE, B_NOPE + B_V
    src_q, dst_q, src_k, dst_k, src_v = [], [], [], [], []
    for hd in range(B_HEADS):
        src_q += list(hd * qd + np.arange(B_NOPE)) + list(hd * qd + B_NOPE + np.arange(B_ROPE))
        dst_q += list(hd * LANES + nope) + list(hd * LANES + pe)
        src_k += list(hd * kvd + np.arange(B_NOPE))
        dst_k += list(hd * LANES + nope)
        src_v += list(hd * kvd + B_NOPE + np.arange(B_V))
    w_q = _scatter_cols(w_q_up, src_q, dst_q, B_HEADS * LANES).astype(BF16)
    w_k = _scatter_cols(w_kv_up, src_k, dst_k, B_HEADS * LANES)
    w_kv = jnp.concatenate([w_k, w_kv_up[:, np.asarray(src_v)]], axis=1).astype(BF16)
    return w_in_p, w_q, w_kv


def _rope_frequencies():
    inv_a = ROPE_THETA ** (-jnp.arange(0, HEAD_DIM, 2, dtype=F32) / HEAD_DIM)
    inv_b = ROPE_THETA ** (-jnp.arange(0, B_ROPE, 2, dtype=F32) / B_ROPE)
    pad = jnp.zeros((LANES - inv_a.shape[0] - inv_b.shape[0],), F32)
    return jnp.concatenate([inv_a, inv_b, pad])[None, :]


def kernel(x, positions, norm_g, ffn_w_gate, ffn_w_up, ffn_w_down, even_w_in, even_q_norm, even_w_q_up, even_kv_norm, even_w_kv_up, even_w_out, odd_w_in, odd_pool_w, odd_pool_scale, odd_w_out, final_norm):
    batch, seq, d = x.shape
    t = batch * seq
    assert seq % A_TILE == 0
    tm = 512
    h = x.reshape(t, d).astype(F32)
    pos = positions.reshape(t, 1).astype(jnp.int32)
    bf = lambda w: w.astype(BF16)
    gain = lambda i, j: norm_g[i, j][None, :].astype(F32)
    wg, wu, wd = bf(ffn_w_gate), bf(ffn_w_up), bf(ffn_w_down)
    ffn = functools.partial(_ffn, wg=wg, wu=wu, wd=wd, tm=tm)

    h = ffn(h, [], None, gain(0, 0), which=(0, 0))
    tabs = _rope_tables(pos, _rope_frequencies(), tm=1024)
    w_in_p, w_q, w_kv = _even_weights(even_w_in[0], even_w_q_up[0], even_w_kv_up[0])
    qa, ka, va, qb, kb, vb = _even_proj(h, gain(0, 1), w_in_p, tabs, even_q_norm[0][None, :].astype(F32), w_q,
                                        even_kv_norm[0][None, :].astype(F32), w_kv, tm=tm)
    out_a = _dilated_attn(qa, ka, va, batch=batch, seq=seq)
    out_b = _mla_attn(qb, kb, vb, batch=batch, seq=seq, tq=1024, tk=512)
    h = ffn(h, [out_a, out_b], bf(even_w_out[0]), gain(0, 2), which=(0, 1))

    h = ffn(h, [], None, gain(1, 0), which=(1, 0))
    out_c, qd, kd, vd = _odd_proj(h, gain(1, 1), bf(odd_w_in[0]), bf(odd_pool_w[0]),
                                  odd_pool_scale[0][None, :].astype(F32), seq=seq, tm=tm)
    out_d = _sb_attn(qd, kd, vd, batch=batch, seq=seq, tq=512, tk=256)
    h = ffn(h, [out_c, out_d], bf(odd_w_out[0]), gain(1, 2), which=(1, 1), final_g=final_norm[None, :].astype(F32))
    return h.reshape(batch, seq, d).astype(x.dtype)
```

```python
import functools

import jax
import jax.numpy as jnp
import numpy as np
from jax import lax
from jax.experimental import pallas as pl
from jax.experimental.pallas import tpu as pltpu

F32 = jnp.float32
BF16 = jnp.bfloat16

HEAD_DIM = 128
LANES = 128
ROPE_THETA = 10000.0
NORM_EPS = 1e-6
A_HEADS = 6
A_DILATIONS = (1, 4, 16)
A_BACK = 128
B_HEADS = 4
B_Q_LORA = 384
B_KV_LORA = 256
B_NOPE = 64
B_ROPE = 32
B_V = 128
C_WINDOWS = (2, 4, 8, 16)
C_GROUP_DIM = 128
C_WIDTH = len(C_WINDOWS) * C_GROUP_DIM
C_HALO = 16
D_HEADS = 4
MASK_VALUE = -1e30
LOG2_E = 1.4426950408889634
VMEM_LIMIT_BYTES = 56 * 1024 * 1024


def _params(*semantics):
    return pltpu.CompilerParams(dimension_semantics=semantics, vmem_limit_bytes=VMEM_LIMIT_BYTES)


def _resident(shape):
    return pl.BlockSpec(shape, lambda *_: (0,) * len(shape), pipeline_mode=pl.Buffered(1))


def _rms(x, g):
    ms = jnp.mean(x * x, axis=-1, keepdims=True)
    return x * lax.rsqrt(ms + NORM_EPS) * g


def _dot(a, b):
    return jnp.dot(a, b, preferred_element_type=F32)


def _dot_nt(a, b):
    return lax.dot_general(a, b, (((1,), (1,)), ((), ())), preferred_element_type=F32)


def _ffn_body(*refs, mix_widths, final_norm):
    h_ref = refs[0]
    n_mix = len(mix_widths)
    mix_refs = refs[1:1 + n_mix]
    base = 1 + n_mix
    wo_ref = refs[base] if n_mix else None
    base += 1 if n_mix else 0
    g_ref, wg_ref, wu_ref, wd_ref = refs[base:base + 4]
    fg_ref = refs[base + 4] if final_norm else None
    o_ref = refs[-1]

    h = h_ref[...]
    row = 0
    for m_ref, width in zip(mix_refs, mix_widths):
        h = h + _dot(m_ref[...], wo_ref[row:row + width, :])
        row += width
    xn = _rms(h, g_ref[...]).astype(BF16)
    gate = _dot(xn, wg_ref[...])
    up = _dot(xn, wu_ref[...])
    act = (gate * jax.nn.sigmoid(gate) * up).astype(BF16)
    y = h + 0.5 * _dot(act, wd_ref[...])
    if final_norm:
        y = _rms(y, fg_ref[...])
    o_ref[...] = y


def _ffn(h, mix, w_out, g, wg, wu, wd, which, final_g=None, *, tm):
    t, d = h.shape
    d_ff = wg.shape[-1]
    row = lambda w: pl.BlockSpec((tm, w), lambda i: (i, 0))
    pick = lambda r, c: pl.BlockSpec((None, None, r, c), lambda i: (*which, 0, 0), pipeline_mode=pl.Buffered(1))
    args, specs = [h], [row(d)]
    for m in mix:
        args.append(m)
        specs.append(row(m.shape[1]))
    if mix:
        args.append(w_out)
        specs.append(_resident(w_out.shape))
    args += [g, wg, wu, wd]
    specs += [_resident((1, d)), pick(d, d_ff), pick(d, d_ff), pick(d_ff, d)]
    if final_g is not None:
        args.append(final_g)
        specs.append(_resident((1, d)))
    return pl.pallas_call(
        functools.partial(_ffn_body, mix_widths=tuple(m.shape[1] for m in mix), final_norm=final_g is not None),
        grid=(t // tm,),
        in_specs=specs,
        out_specs=row(d),
        out_shape=jax.ShapeDtypeStruct((t, d), F32),
        compiler_params=_params("parallel"),
        name="ffn",
    )(*args)


def _rope_tables(pos, inv):
    lane = lax.broadcasted_iota(jnp.int32, (1, LANES), 1)
    half = LANES // 2
    sign = jnp.where(lane < half, -1.0, 1.0).astype(F32)
    ang = pos * inv
    c, s = jnp.cos(ang), jnp.sin(ang)
    c_sw, s_sw = pltpu.roll(c, half, 1), pltpu.roll(s, half, 1)
    first, second = lane < B_ROPE // 2, (lane >= half) & (lane < half + B_ROPE // 2)
    return (jnp.where(lane < half, c, c_sw), sign * jnp.where(lane < half, s, s_sw),
            jnp.where(first, c_sw, jnp.where(second, c, 1.0)), sign * jnp.where(first, s_sw, jnp.where(second, s, 0.0)))


def _rotate(x, c, s):
    return x * c + pltpu.roll(x, LANES // 2, 1) * s


def _even_proj_body(h_ref, g_ref, win_ref, pos_ref, inv_ref, qn_ref, wq_ref, kvn_ref, wkv_ref,
                    qa_ref, ka_ref, va_ref, qb_ref, kb_ref, vb_ref):
    xn = _rms(h_ref[...], g_ref[...]).astype(BF16)
    proj = _dot(xn, win_ref[...])
    ca, sa, cb, sb = _rope_tables(pos_ref[...].astype(F32), inv_ref[...])
    a_w = A_HEADS * HEAD_DIM
    blk = lambda base, i: slice(base + i * LANES, base + (i + 1) * LANES)
    for hd in range(A_HEADS):
        qa_ref[hd] = (_rotate(proj[:, blk(0, hd)], ca, sa) * (HEAD_DIM ** -0.5 * LOG2_E)).astype(BF16)
        ka_ref[hd] = _rotate(proj[:, blk(a_w, hd)], ca, sa).astype(BF16)
        va_ref[hd] = proj[:, blk(2 * a_w, hd)].astype(BF16)
    o_cq = 3 * a_w
    o_ckv = o_cq + B_Q_LORA
    o_kr = o_ckv + B_KV_LORA
    cq = _rms(proj[:, o_cq:o_ckv], qn_ref[...]).astype(BF16)
    ckv = _rms(proj[:, o_ckv:o_kr], kvn_ref[...]).astype(BF16)
    qb = _dot(cq, wq_ref[...])
    kv = _dot(ckv, wkv_ref[...])
    k_pe = _rotate(proj[:, o_kr:o_kr + LANES], cb, sb)
    q_scale = (B_NOPE + B_ROPE) ** -0.5 * LOG2_E
    for hd in range(B_HEADS):
        qb_ref[hd] = (_rotate(qb[:, blk(0, hd)], cb, sb) * q_scale).astype(BF16)
        kb_ref[hd] = (kv[:, blk(0, hd)] + k_pe).astype(BF16)
        vb_ref[hd] = kv[:, blk(B_HEADS * LANES, hd)].astype(BF16)


def _even_proj(h, g, w_in, pos, inv, q_norm, w_q, kv_norm, w_kv, *, tm):
    t, d = h.shape
    row = lambda w: pl.BlockSpec((tm, w), lambda i: (i, 0))
    heads = lambda n: pl.BlockSpec((n, tm, LANES), lambda i: (0, i, 0))
    shp = lambda n: jax.ShapeDtypeStruct((n, t, LANES), BF16)
    return pl.pallas_call(
        _even_proj_body,
        grid=(t // tm,),
        in_specs=[row(d), _resident((1, d)), _resident(w_in.shape), row(1), _resident(inv.shape),
                  _resident(q_norm.shape), _resident(w_q.shape), _resident(kv_norm.shape), _resident(w_kv.shape)],
        out_specs=[heads(A_HEADS)] * 3 + [heads(B_HEADS)] * 3,
        out_shape=[shp(A_HEADS)] * 3 + [shp(B_HEADS)] * 3,
        compiler_params=_params("parallel"),
        name="even_proj",
    )(h, g, w_in, pos, inv, q_norm, w_q, kv_norm, w_kv)


A_TILE = A_BACK * max(A_DILATIONS)
A_STEP = 4
assert A_DILATIONS == (1, A_STEP, A_STEP * A_STEP)
A_PHASE4 = A_TILE // A_STEP


def _dilated_body(q_ref, k_ref, v_ref, o_ref, nat, p4f, q4, q16, k1, k4, k16, v1, v4, v16, o_s, lse_s, bias):
    i = pl.program_id(2)

    @pl.when(i > 0)
    def _():
        for buf in (k1, v1):
            buf[0:A_BACK] = buf[A_TILE:A_TILE + A_BACK]
        for buf in (k4, v4):
            buf[:, 0:A_BACK] = buf[:, A_PHASE4:A_PHASE4 + A_BACK]
        for buf in (k16, v16):
            buf[:, 0:A_BACK] = buf[:, A_BACK:2 * A_BACK]

    @pl.when(i == 0)
    def _():
        for buf in (k1, v1):
            buf[0:A_BACK] = jnp.zeros((A_BACK, HEAD_DIM), BF16)
        for buf in (k4, v4, k16, v16):
            buf[:, 0:A_BACK] = jnp.zeros((buf.shape[0], A_BACK, HEAD_DIM), BF16)

    def regroup(x_ref, d4, d16, halo):
        nat[...] = x_ref[0].astype(F32)
        for r4 in range(A_STEP):
            rows = nat[pl.ds(r4, A_PHASE4, stride=A_STEP), :]
            p4f[r4 * A_PHASE4:(r4 + 1) * A_PHASE4] = rows
            d4[r4, halo:halo + A_PHASE4] = rows.astype(BF16)
        for r16 in range(A_STEP * A_STEP):
            r4, sub = r16 % A_STEP, r16 // A_STEP
            rows = p4f[pl.ds(r4 * A_PHASE4 + sub, A_BACK, stride=A_STEP), :]
            d16[r16, halo:halo + A_BACK] = rows.astype(BF16)

    regroup(q_ref, q4, q16, 0)
    k1[A_BACK:A_BACK + A_TILE] = k_ref[0]
    regroup(k_ref, k4, k16, A_BACK)
    v1[A_BACK:A_BACK + A_TILE] = v_ref[0]
    regroup(v_ref, v4, v16, A_BACK)

    qi = lax.broadcasted_iota(jnp.int32, (A_BACK, 2 * A_BACK), 0)
    ki = lax.broadcasted_iota(jnp.int32, (A_BACK, 2 * A_BACK), 1)
    band = (ki >= qi) & (ki <= qi + A_BACK)
    bias[0] = jnp.where(band, 0.0, MASK_VALUE)
    bias[1] = jnp.where(band & (ki >= jnp.where(i > 0, 0, A_BACK)), 0.0, MASK_VALUE)

    def band_block(p, qb, kb, vb, first, out_rows):
        s = _dot_nt(qb, kb) + bias[1 if first else 0]
        m = jnp.max(s, axis=-1, keepdims=True)
        e = jnp.exp2(s - m)
        den = jnp.sum(e, axis=-1, keepdims=True)
        o_s[p, out_rows, :] = _dot(e.astype(BF16), vb) / den
        lse_s[p, out_rows, :] = jnp.broadcast_to(m + jnp.log2(den), (A_BACK, HEAD_DIM))

    for j in range(A_TILE // A_BACK):
        blk, bnd = slice(j * A_BACK, (j + 1) * A_BACK), slice(j * A_BACK, (j + 2) * A_BACK)
        band_block(0, q_ref[0, blk, :], k1[bnd], v1[bnd], j == 0, blk)
    for r4 in range(A_STEP):
        for j in range(A_PHASE4 // A_BACK):
            blk, bnd = slice(j * A_BACK, (j + 1) * A_BACK), slice(j * A_BACK, (j + 2) * A_BACK)
            band_block(1, q4[r4, blk], k4[r4, bnd], v4[r4, bnd], j == 0,
                       pl.ds(r4 + j * A_BACK * A_STEP, A_BACK, stride=A_STEP))
    for r16 in range(A_STEP * A_STEP):
        band_block(2, q16[r16], k16[r16], v16[r16], True, pl.ds(r16, A_BACK, stride=A_STEP * A_STEP))

    n_pat = len(A_DILATIONS)
    lse = [lse_s[p] for p in range(n_pat)]
    top = functools.reduce(jnp.maximum, lse)
    wts = [jnp.exp2(x - top) for x in lse]
    total = functools.reduce(jnp.add, wts)
    out = functools.reduce(jnp.add, [w * o_s[p] for p, w in enumerate(wts)]) / total
    o_ref[...] = out.astype(BF16)


def _dilated_attn(q, k, v, *, batch, seq):
    n_tiles = seq // A_TILE
    heads, t, _ = q.shape
    blk = pl.BlockSpec((1, A_TILE, HEAD_DIM), lambda b, h, i: (h, b * n_tiles + i, 0))
    n16 = A_STEP * A_STEP
    return pl.pallas_call(
        _dilated_body,
        grid=(batch, heads, n_tiles),
        in_specs=[blk, blk, blk],
        out_specs=pl.BlockSpec((A_TILE, HEAD_DIM), lambda b, h, i: (b * n_tiles + i, h)),
        out_shape=jax.ShapeDtypeStruct((t, heads * HEAD_DIM), BF16),
        scratch_shapes=[
            pltpu.VMEM((A_TILE, HEAD_DIM), F32),
            pltpu.VMEM((A_TILE, HEAD_DIM), F32),
            pltpu.VMEM((A_STEP, A_PHASE4, HEAD_DIM), BF16),
            pltpu.VMEM((n16, A_BACK, HEAD_DIM), BF16),
        ] + [
            pltpu.VMEM((A_BACK + A_TILE, HEAD_DIM), BF16),
            pltpu.VMEM((A_STEP, A_BACK + A_PHASE4, HEAD_DIM), BF16),
            pltpu.VMEM((n16, 2 * A_BACK, HEAD_DIM), BF16),
        ] * 2 + [
            pltpu.VMEM((len(A_DILATIONS), A_TILE, HEAD_DIM), F32),
            pltpu.VMEM((len(A_DILATIONS), A_TILE, HEAD_DIM), F32),
            pltpu.VMEM((2, A_BACK, 2 * A_BACK), F32),
        ],
        compiler_params=_params("parallel", "parallel", "arbitrary"),
        name="dilated_attn",
    )(q, k, v)


def _mla_body(q_ref, k_ref, v_ref, o_ref, s_a, s_b, mx_a, mx_b, m_s, l_s, acc_s, *, tq, tk):
    qi = pl.program_id(2)
    n_tiles = tk // LANES
    n_blocks = 2 * qi + 2
    m_s[...] = jnp.full(m_s.shape, MASK_VALUE, F32)
    l_s[...] = jnp.zeros(l_s.shape, F32)
    acc_s[...] = jnp.zeros(acc_s.shape, F32)

    def key_rows(n):
        return pl.ds(pl.multiple_of(n * tk, tk), tk)

    def logits(n, s_ref, mx_ref, diag_block):
        s = _dot_nt(q_ref[0], k_ref[0, key_rows(n), :])
        tiles = [s[:, c * LANES:(c + 1) * LANES] for c in range(n_tiles)]
        if diag_block is not None:
            row = lax.broadcasted_iota(jnp.int32, (tq, LANES), 0)
            col = lax.broadcasted_iota(jnp.int32, (tq, LANES), 1) + diag_block * tk
            tiles = [jnp.where(col + c * LANES <= row, x, MASK_VALUE) for c, x in enumerate(tiles)]
        for c, x in enumerate(tiles):
            s_ref[:, c * LANES:(c + 1) * LANES] = x
        mx = jnp.max(functools.reduce(jnp.maximum, tiles), axis=-1, keepdims=True)
        mx_ref[...] = jnp.broadcast_to(mx, mx_ref.shape)

    def softmax_av(s_ref, mx_ref, n):
        m_prev = m_s[...]
        m_new = jnp.maximum(m_prev, mx_ref[...])
        alpha = jnp.exp2(m_prev - m_new)
        p = [jnp.exp2(s_ref[:, c * LANES:(c + 1) * LANES] - m_new) for c in range(n_tiles)]
        l_s[...] = alpha * l_s[...] + functools.reduce(jnp.add, p)
        pb = jnp.concatenate([x.astype(BF16) for x in p], axis=1)
        acc_s[...] = alpha * acc_s[...] + _dot(pb, v_ref[0, key_rows(n), :])
        m_s[...] = m_new

    @pl.when(qi > 0)
    def _():
        logits(0, s_a, mx_a, None)

    def pair(u, carry):
        logits(2 * u + 1, s_b, mx_b, None)
        softmax_av(s_a, mx_a, 2 * u)
        logits(2 * u + 2, s_a, mx_a, None)
        softmax_av(s_b, mx_b, 2 * u + 1)
        return carry

    lax.fori_loop(0, qi - 1, pair, 0)

    @pl.when(qi > 0)
    def _():
        logits(n_blocks - 3, s_b, mx_b, None)
        softmax_av(s_a, mx_a, n_blocks - 4)
        logits(n_blocks - 2, s_a, mx_a, 0)
        softmax_av(s_b, mx_b, n_blocks - 3)

    @pl.when(qi == 0)
    def _():
        logits(0, s_a, mx_a, 0)

    logits(n_blocks - 1, s_b, mx_b, 1)
    softmax_av(s_a, mx_a, n_blocks - 2)
    softmax_av(s_b, mx_b, n_blocks - 1)
    o_ref[...] = (acc_s[...] / jnp.sum(l_s[...], axis=-1, keepdims=True)).astype(BF16)


def _mla_attn(q, k, v, *, batch, seq, tq, tk):
    assert tq == 2 * tk
    heads, t, _ = q.shape
    nq = seq // tq
    kv = pl.BlockSpec((1, seq, LANES), lambda h, b, i: (h, b, 0))
    return pl.pallas_call(
        functools.partial(_mla_body, tq=tq, tk=tk),
        grid=(heads, batch, nq),
        in_specs=[pl.BlockSpec((1, tq, LANES), lambda h, b, i: (h, b * nq + i, 0)), kv, kv],
        out_specs=pl.BlockSpec((tq, LANES), lambda h, b, i: (b * nq + i, h)),
        out_shape=jax.ShapeDtypeStruct((t, heads * LANES), BF16),
        scratch_shapes=[pltpu.VMEM((tq, tk), F32)] * 2 + [pltpu.VMEM((tq, LANES), F32)] * 5,
        compiler_params=_params("parallel", "parallel", "arbitrary"),
        name="mla_attn",
    )(q, k, v)


def _odd_proj_body(h_ref, g_ref, win_ref, pw_ref, ps_ref, oc_ref, qd_ref, kd_ref, vd_ref, ext, *, tm, tiles_per_seq):
    i = pl.program_id(0)
    xn = _rms(h_ref[...], g_ref[...]).astype(BF16)
    proj = _dot(xn, win_ref[...])

    @pl.when(i % tiles_per_seq == 0)
    def _():
        ext[0:C_HALO, :] = jnp.zeros((C_HALO, C_WIDTH), F32)

    ext[C_HALO:C_HALO + tm, :] = proj[:, 0:C_WIDTH]
    t_in_seq = (i % tiles_per_seq) * tm + lax.broadcasted_iota(jnp.int32, (tm, 1), 0)
    for g, w in enumerate(C_WINDOWS):
        cols = slice(g * C_GROUP_DIM, (g + 1) * C_GROUP_DIM)
        s = ext[C_HALO - (w - 1):C_HALO + tm, cols]
        step = 1
        while step < w:
            s = s[step:] + s[:-step]
            step *= 2
        count = jnp.minimum(t_in_seq + 1, w).astype(F32)
        pooled = s / count - proj[:, cols]
        oc_ref[:, cols] = (_dot(pooled.astype(BF16), pw_ref[g]) * ps_ref[:, cols]).astype(BF16)
    ext[0:C_HALO, :] = ext[tm:tm + C_HALO, :]

    dw = D_HEADS * HEAD_DIM
    for hd in range(D_HEADS):
        lanes = lambda base: slice(base + hd * LANES, base + (hd + 1) * LANES)
        qd_ref[hd] = (proj[:, lanes(C_WIDTH)] * (HEAD_DIM ** -0.5 * LOG2_E)).astype(BF16)
        kd_ref[hd] = proj[:, lanes(C_WIDTH + dw)].astype(BF16)
        vd_ref[hd] = proj[:, lanes(C_WIDTH + 2 * dw)].astype(BF16)


def _odd_proj(h, g, w_in, pool_w, pool_scale, *, seq, tm):
    t, d = h.shape
    row = lambda w: pl.BlockSpec((tm, w), lambda i: (i, 0))
    heads = pl.BlockSpec((D_HEADS, tm, LANES), lambda i: (0, i, 0))
    shp = jax.ShapeDtypeStruct((D_HEADS, t, LANES), BF16)
    return pl.pallas_call(
        functools.partial(_odd_proj_body, tm=tm, tiles_per_seq=seq // tm),
        grid=(t // tm,),
        in_specs=[row(d), _resident((1, d)), _resident(w_in.shape), _resident(pool_w.shape), _resident(pool_scale.shape)],
        out_specs=[row(C_WIDTH), heads, heads, heads],
        out_shape=[jax.ShapeDtypeStruct((t, C_WIDTH), BF16), shp, shp, shp],
        scratch_shapes=[pltpu.VMEM((C_HALO + tm, C_WIDTH), F32)],
        compiler_params=_params("arbitrary"),
        name="odd_proj",
    )(h, g, w_in, pool_w, pool_scale)


SB_SUB = 256
SB_DEAD_CARRY = 160.0


def _sb_body(q_ref, k_ref, v_ref, o_ref, z_a, z_b, e_a, e_b, carry_s, acc_s, *, tq, tk):
    qi = pl.program_id(2)
    n_blocks = 2 * qi + 2
    q = q_ref[0]
    carry_s[...] = jnp.zeros(carry_s.shape, F32)
    acc_s[...] = jnp.zeros(acc_s.shape, F32)
    later = (lax.broadcasted_iota(jnp.int32, (SB_SUB, SB_SUB), 0) > lax.broadcasted_iota(jnp.int32, (SB_SUB, SB_SUB), 1))
    later = jnp.where(later, 1.0, 0.0).astype(BF16)

    def key_rows(n):
        kb = jnp.maximum(n_blocks - 1 - n, 0)
        return pl.ds(pl.multiple_of(kb * tk, tk), tk)

    def logits(n, z_ref):
        z_ref[...] = _dot_nt(q, k_ref[0, key_rows(n), :])

    def gate(z_ref, e_ref, diag_block):
        carry = carry_s[...]
        for sub in reversed(range(tk // SB_SUB)):
            z = z_ref[:, sub * SB_SUB:(sub + 1) * SB_SUB]
            softplus = jnp.maximum(z, 0.0) + jnp.log2(1.0 + jnp.exp2(-jnp.abs(z)))
            arg = z - softplus
            if diag_block is None:
                keep = softplus
            else:
                row = lax.broadcasted_iota(jnp.int32, (tq, SB_SUB), 0)
                col = lax.broadcasted_iota(jnp.int32, (tq, SB_SUB), 1) + diag_block * tk + sub * SB_SUB
                keep = jnp.where(col < row, softplus, 0.0)
                arg = jnp.where(col < row, arg, MASK_VALUE)
            expo = arg - _dot(keep.astype(BF16), later)
            for c in range(SB_SUB // LANES):
                lo = sub * SB_SUB + c * LANES
                e_ref[:, lo:lo + LANES] = expo[:, c * LANES:(c + 1) * LANES] - carry
            carry = carry + jnp.sum(keep, axis=-1, keepdims=True)
        carry_s[...] = carry

    def weigh(e_ref, n):
        acc_s[...] += _dot(jnp.exp2(e_ref[...]).astype(BF16), v_ref[0, key_rows(n), :])

    logits(0, z_a)
    logits(1, z_b)
    gate(z_a, e_a, 1)
    logits(2, z_a)
    gate(z_b, e_b, 0)
    weigh(e_a, 0)

    def step(n, z_this, e_this, z_next, e_prev):
        logits(n + 1, z_next)
        gate(z_this, e_this, None)
        weigh(e_prev, n - 1)

    def one_block(state):
        n, _ = state

        @pl.when(n % 2 == 0)
        def _():
            step(n, z_a, e_a, z_b, e_b)

        @pl.when(n % 2 == 1)
        def _():
            step(n, z_b, e_b, z_a, e_a)

        return n + 1, jnp.min(carry_s[...])

    def more_blocks(state):
        n, min_carry = state
        return (n < n_blocks) & (min_carry < SB_DEAD_CARRY)

    n_done, _ = lax.while_loop(more_blocks, one_block, (jnp.int32(2), jnp.min(carry_s[...])))

    @pl.when(n_done % 2 == 0)
    def _():
        weigh(e_b, n_done - 1)

    @pl.when(n_done % 2 == 1)
    def _():
        weigh(e_a, n_done - 1)

    o_ref[...] = acc_s[...].astype(BF16)


def _sb_attn(q, k, v, *, batch, seq, tq, tk):
    assert tq == 2 * tk and tk % SB_SUB == 0
    heads, t, _ = q.shape
    nq = seq // tq
    kv = pl.BlockSpec((1, seq, LANES), lambda h, b, i: (h, b, 0))
    return pl.pallas_call(
        functools.partial(_sb_body, tq=tq, tk=tk),
        grid=(heads, batch, nq),
        in_specs=[pl.BlockSpec((1, tq, LANES), lambda h, b, i: (h, b * nq + i, 0)), kv, kv],
        out_specs=pl.BlockSpec((tq, LANES), lambda h, b, i: (b * nq + i, h)),
        out_shape=jax.ShapeDtypeStruct((t, heads * LANES), BF16),
        scratch_shapes=[pltpu.VMEM((tq, tk), F32)] * 4 + [pltpu.VMEM((tq, LANES), F32)] * 2,
        compiler_params=_params("parallel", "parallel", "arbitrary"),
        name="sb_attn",
    )(q, k, v)


def _latent_head_lanes():
    half_pe, half_nope = B_ROPE // 2, B_NOPE // 2
    pe = np.concatenate([np.arange(half_pe), LANES // 2 + np.arange(half_pe)])
    nope = np.concatenate([half_pe + np.arange(half_nope), LANES // 2 + half_pe + np.arange(half_nope)])
    return pe, nope


def _scatter_cols(w, src_cols, dst_cols, width):
    out = jnp.zeros((w.shape[0], width), w.dtype)
    return out.at[:, np.asarray(dst_cols)].set(w[:, np.asarray(src_cols)])


def _even_weights(w_in, w_q_up, w_kv_up):
    pe, nope = _latent_head_lanes()
    a_w = A_HEADS * HEAD_DIM
    o_kr = 3 * a_w + B_Q_LORA + B_KV_LORA
    k_rope = _scatter_cols(w_in, o_kr + np.arange(B_ROPE), pe, LANES)
    w_in_p = jnp.concatenate([w_in[:, :o_kr], k_rope], axis=1).astype(BF16)
    qd, kvd = B_NOPE + B_ROPE, B_NOPE + B_V
    src_q, dst_q, src_k, dst_k, src_v = [], [], [], [], []
    for hd in range(B_HEADS):
        src_q += list(hd * qd + np.arange(B_NOPE)) + list(hd * qd + B_NOPE + np.arange(B_ROPE))
        dst_q += list(hd * LANES + nope) + list(hd * LANES + pe)
        src_k += list(hd * kvd + np.arange(B_NOPE))
        dst_k += list(hd * LANES + nope)
        src_v += list(hd * kvd + B_NOPE + np.arange(B_V))
    w_q = _scatter_cols(w_q_up, src_q, dst_q, B_HEADS * LANES).astype(BF16)
    w_k = _scatter_cols(w_kv_up, src_k, dst_k, B_HEADS * LANES)
    w_kv = jnp.concatenate([w_k, w_kv_up[:, np.asarray(src_v)]], axis=1).astype(BF16)
    return w_in_p, w_q, w_kv


def _rope_frequencies():
    inv_a = ROPE_THETA ** (-jnp.arange(0, HEAD_DIM, 2, dtype=F32) / HEAD_DIM)
    inv_b = ROPE_THETA ** (-jnp.arange(0, B_ROPE, 2, dtype=F32) / B_ROPE)
    pad = jnp.zeros((LANES - inv_a.shape[0] - inv_b.shape[0],), F32)
    return jnp.concatenate([inv_a, inv_b, pad])[None, :]


def kernel(x, positions, norm_g, ffn_w_gate, ffn_w_up, ffn_w_down, even_w_in, even_q_norm, even_w_q_up, even_kv_norm, even_w_kv_up, even_w_out, odd_w_in, odd_pool_w, odd_pool_scale, odd_w_out, final_norm):
    batch, seq, d = x.shape
    t = batch * seq
    assert seq % A_TILE == 0
    tm = 512
    h = x.reshape(t, d).astype(F32)
    pos = positions.reshape(t, 1).astype(jnp.int32)
    bf = lambda w: w.astype(BF16)
    gain = lambda i, j: norm_g[i, j][None, :].astype(F32)
    wg, wu, wd = bf(ffn_w_gate), bf(ffn_w_up), bf(ffn_w_down)
    ffn = functools.partial(_ffn, wg=wg, wu=wu, wd=wd, tm=tm)

    h = ffn(h, [], None, gain(0, 0), which=(0, 0))
    w_in_p, w_q, w_kv = _even_weights(even_w_in[0], even_w_q_up[0], even_w_kv_up[0])
    qa, ka, va, qb, kb, vb = _even_proj(h, gain(0, 1), w_in_p, pos, _rope_frequencies(), even_q_norm[0][None, :].astype(F32), w_q,
                                        even_kv_norm[0][None, :].astype(F32), w_kv, tm=tm)
    out_a = _dilated_attn(qa, ka, va, batch=batch, seq=seq)
    out_b = _mla_attn(qb, kb, vb, batch=batch, seq=seq, tq=1024, tk=512)
    h = ffn(h, [out_a, out_b], bf(even_w_out[0]), gain(0, 2), which=(0, 1))

    h = ffn(h, [], None, gain(1, 0), which=(1, 0))
    out_c, qd, kd, vd = _odd_proj(h, gain(1, 1), bf(odd_w_in[0]), bf(odd_pool_w[0]),
                                  odd_pool_scale[0][None, :].astype(F32), seq=seq, tm=tm)
    out_d = _sb_attn(qd, kd, vd, batch=batch, seq=seq, tq=512, tk=256)
    h = ffn(h, [out_c, out_d], bf(odd_w_out[0]), gain(1, 2), which=(1, 1), final_g=final_norm[None, :].astype(F32))
    return h.reshape(batch, seq, d).astype(x.dtype)
```

```python
import functools

import jax
import jax.numpy as jnp
import numpy as np
from jax import lax
from jax.experimental import pallas as pl
from jax.experimental.pallas import tpu as pltpu

F32 = jnp.float32
BF16 = jnp.bfloat16

HEAD_DIM = 128
LANES = 128
ROPE_THETA = 10000.0
NORM_EPS = 1e-6
A_HEADS = 6
A_DILATIONS = (1, 4, 16)
A_BACK = 128
B_HEADS = 4
B_Q_LORA = 384
B_KV_LORA = 256
B_NOPE = 64
B_ROPE = 32
B_V = 128
C_WINDOWS = (2, 4, 8, 16)
C_GROUP_DIM = 128
C_WIDTH = len(C_WINDOWS) * C_GROUP_DIM
C_HALO = 16
D_HEADS = 4
MASK_VALUE = -1e30
LOG2_E = 1.4426950408889634
VMEM_LIMIT_BYTES = 56 * 1024 * 1024


def _params(*semantics):
    return pltpu.CompilerParams(dimension_semantics=semantics, vmem_limit_bytes=VMEM_LIMIT_BYTES)


def _resident(shape):
    return pl.BlockSpec(shape, lambda *_: (0,) * len(shape), pipeline_mode=pl.Buffered(1))


def _rms(x, g):
    ms = jnp.mean(x * x, axis=-1, keepdims=True)
    return x * lax.rsqrt(ms + NORM_EPS) * g


def _dot(a, b):
    return jnp.dot(a, b, preferred_element_type=F32)


def _dot_nt(a, b):
    return lax.dot_general(a, b, (((1,), (1,)), ((), ())), preferred_element_type=F32)


def _ffn_body(*refs, mix_widths, final_norm):
    h_ref = refs[0]
    n_mix = len(mix_widths)
    mix_refs = refs[1:1 + n_mix]
    base = 1 + n_mix
    wo_ref = refs[base] if n_mix else None
    base += 1 if n_mix else 0
    g_ref, wg_ref, wu_ref, wd_ref = refs[base:base + 4]
    fg_ref = refs[base + 4] if final_norm else None
    o_ref = refs[-1]

    h = h_ref[...]
    row = 0
    for m_ref, width in zip(mix_refs, mix_widths):
        h = h + _dot(m_ref[...], wo_ref[row:row + width, :])
        row += width
    xn = _rms(h, g_ref[...]).astype(BF16)
    gate = _dot(xn, wg_ref[...])
    up = _dot(xn, wu_ref[...])
    act = (gate * jax.nn.sigmoid(gate) * up).astype(BF16)
    y = h + 0.5 * _dot(act, wd_ref[...])
    if final_norm:
        y = _rms(y, fg_ref[...])
    o_ref[...] = y


def _ffn(h, mix, w_out, g, wg, wu, wd, which, final_g=None, *, tm):
    t, d = h.shape
    d_ff = wg.shape[-1]
    row = lambda w: pl.BlockSpec((tm, w), lambda i: (i, 0))
    pick = lambda r, c: pl.BlockSpec((None, None, r, c), lambda i: (*which, 0, 0), pipeline_mode=pl.Buffered(1))
    args, specs = [h], [row(d)]
    for m in mix:
        args.append(m)
        specs.append(row(m.shape[1]))
    if mix:
        args.append(w_out)
        specs.append(_resident(w_out.shape))
    args += [g, wg, wu, wd]
    specs += [_resident((1, d)), pick(d, d_ff), pick(d, d_ff), pick(d_ff, d)]
    if final_g is not None:
        args.append(final_g)
        specs.append(_resident((1, d)))
    return pl.pallas_call(
        functools.partial(_ffn_body, mix_widths=tuple(m.shape[1] for m in mix), final_norm=final_g is not None),
        grid=(t // tm,),
        in_specs=specs,
        out_specs=row(d),
        out_shape=jax.ShapeDtypeStruct((t, d), F32),
        compiler_params=_params("parallel"),
        name="ffn",
    )(*args)


def _rope_tables(pos, inv):
    lane = lax.broadcasted_iota(jnp.int32, (1, LANES), 1)
    half = LANES // 2
    sign = jnp.where(lane < half, -1.0, 1.0).astype(F32)
    ang = pos * inv
    c, s = jnp.cos(ang), jnp.sin(ang)
    c_sw, s_sw = pltpu.roll(c, half, 1), pltpu.roll(s, half, 1)
    first, second = lane < B_ROPE // 2, (lane >= half) & (lane < half + B_ROPE // 2)
    return (jnp.where(lane < half, c, c_sw), sign * jnp.where(lane < half, s, s_sw),
            jnp.where(first, c_sw, jnp.where(second, c, 1.0)), sign * jnp.where(first, s_sw, jnp.where(second, s, 0.0)))


def _rotate(x, c, s):
    return x * c + pltpu.roll(x, LANES // 2, 1) * s


def _even_proj_body(h_ref, g_ref, win_ref, pos_ref, inv_ref, qn_ref, wq_ref, kvn_ref, wkv_ref,
                    qa_ref, ka_ref, va_ref, qb_ref, kb_ref, vb_ref):
    xn = _rms(h_ref[...], g_ref[...]).astype(BF16)
    proj = _dot(xn, win_ref[...])
    ca, sa, cb, sb = _rope_tables(pos_ref[...].astype(F32), inv_ref[...])
    a_w = A_HEADS * HEAD_DIM
    blk = lambda base, i: slice(base + i * LANES, base + (i + 1) * LANES)
    for hd in range(A_HEADS):
        qa_ref[hd] = (_rotate(proj[:, blk(0, hd)], ca, sa) * (HEAD_DIM ** -0.5 * LOG2_E)).astype(BF16)
        ka_ref[hd] = _rotate(proj[:, blk(a_w, hd)], ca, sa).astype(BF16)
        va_ref[hd] = proj[:, blk(2 * a_w, hd)].astype(BF16)
    o_cq = 3 * a_w
    o_ckv = o_cq + B_Q_LORA
    o_kr = o_ckv + B_KV_LORA
    cq = _rms(proj[:, o_cq:o_ckv], qn_ref[...]).astype(BF16)
    ckv = _rms(proj[:, o_ckv:o_kr], kvn_ref[...]).astype(BF16)
    qb = _dot(cq, wq_ref[...])
    kv = _dot(ckv, wkv_ref[...])
    k_pe = _rotate(proj[:, o_kr:o_kr + LANES], cb, sb)
    q_scale = (B_NOPE + B_ROPE) ** -0.5 * LOG2_E
    for hd in range(B_HEADS):
        qb_ref[hd] = (_rotate(qb[:, blk(0, hd)], cb, sb) * q_scale).astype(BF16)
        kb_ref[hd] = (kv[:, blk(0, hd)] + k_pe).astype(BF16)
        vb_ref[hd] = kv[:, blk(B_HEADS * LANES, hd)].astype(BF16)


def _even_proj(h, g, w_in, pos, inv, q_norm, w_q, kv_norm, w_kv, *, tm):
    t, d = h.shape
    row = lambda w: pl.BlockSpec((tm, w), lambda i: (i, 0))
    heads = lambda n: pl.BlockSpec((n, tm, LANES), lambda i: (0, i, 0))
    shp = lambda n: jax.ShapeDtypeStruct((n, t, LANES), BF16)
    return pl.pallas_call(
        _even_proj_body,
        grid=(t // tm,),
        in_specs=[row(d), _resident((1, d)), _resident(w_in.shape), row(1), _resident(inv.shape),
                  _resident(q_norm.shape), _resident(w_q.shape), _resident(kv_norm.shape), _resident(w_kv.shape)],
        out_specs=[heads(A_HEADS)] * 3 + [heads(B_HEADS)] * 3,
        out_shape=[shp(A_HEADS)] * 3 + [shp(B_HEADS)] * 3,
        compiler_params=_params("parallel"),
        name="even_proj",
    )(h, g, w_in, pos, inv, q_norm, w_q, kv_norm, w_kv)


A_TILE = A_BACK * max(A_DILATIONS)
A_STEP = 4
assert A_DILATIONS == (1, A_STEP, A_STEP * A_STEP)
A_PHASE4 = A_TILE // A_STEP


def _dilated_body(q_ref, k_ref, v_ref, o_ref, nat, p4f, q4, q16, k1, k4, k16, v1, v4, v16, o_s, lse_s, bias):
    i = pl.program_id(2)

    @pl.when(i > 0)
    def _():
        for buf in (k1, v1):
            buf[0:A_BACK] = buf[A_TILE:A_TILE + A_BACK]
        for buf in (k4, v4):
            buf[:, 0:A_BACK] = buf[:, A_PHASE4:A_PHASE4 + A_BACK]
        for buf in (k16, v16):
            buf[:, 0:A_BACK] = buf[:, A_BACK:2 * A_BACK]

    @pl.when(i == 0)
    def _():
        for buf in (k1, v1):
            buf[0:A_BACK] = jnp.zeros((A_BACK, HEAD_DIM), BF16)
        for buf in (k4, v4, k16, v16):
            buf[:, 0:A_BACK] = jnp.zeros((buf.shape[0], A_BACK, HEAD_DIM), BF16)

    def regroup(x_ref, d4, d16, halo):
        nat[...] = x_ref[0].astype(F32)
        for r4 in range(A_STEP):
            rows = nat[pl.ds(r4, A_PHASE4, stride=A_STEP), :]
            p4f[r4 * A_PHASE4:(r4 + 1) * A_PHASE4] = rows
            d4[r4, halo:halo + A_PHASE4] = rows.astype(BF16)
        for r16 in range(A_STEP * A_STEP):
            r4, sub = r16 % A_STEP, r16 // A_STEP
            rows = p4f[pl.ds(r4 * A_PHASE4 + sub, A_BACK, stride=A_STEP), :]
            d16[r16, halo:halo + A_BACK] = rows.astype(BF16)

    regroup(q_ref, q4, q16, 0)
    k1[A_BACK:A_BACK + A_TILE] = k_ref[0]
    regroup(k_ref, k4, k16, A_BACK)
    v1[A_BACK:A_BACK + A_TILE] = v_ref[0]
    regroup(v_ref, v4, v16, A_BACK)

    qi = lax.broadcasted_iota(jnp.int32, (A_BACK, 2 * A_BACK), 0)
    ki = lax.broadcasted_iota(jnp.int32, (A_BACK, 2 * A_BACK), 1)
    band = (ki >= qi) & (ki <= qi + A_BACK)
    bias[0] = jnp.where(band, 0.0, MASK_VALUE)
    bias[1] = jnp.where(band & (ki >= jnp.where(i > 0, 0, A_BACK)), 0.0, MASK_VALUE)

    def band_block(p, qb, kb, vb, first, out_rows):
        s = _dot_nt(qb, kb) + bias[1 if first else 0]
        m = jnp.max(s, axis=-1, keepdims=True)
        e = jnp.exp2(s - m)
        den = jnp.sum(e, axis=-1, keepdims=True)
        o_s[p, out_rows, :] = _dot(e.astype(BF16), vb) / den
        lse_s[p, out_rows, :] = jnp.broadcast_to(m + jnp.log2(den), (A_BACK, HEAD_DIM))

    @pl.when(i >= 0)
    def _():
        for j in range(A_TILE // A_BACK):
            blk, bnd = slice(j * A_BACK, (j + 1) * A_BACK), slice(j * A_BACK, (j + 2) * A_BACK)
            band_block(0, q_ref[0, blk, :], k1[bnd], v1[bnd], j == 0, blk)
        for r4 in range(A_STEP):
            for j in range(A_PHASE4 // A_BACK):
                blk, bnd = slice(j * A_BACK, (j + 1) * A_BACK), slice(j * A_BACK, (j + 2) * A_BACK)
                band_block(1, q4[r4, blk], k4[r4, bnd], v4[r4, bnd], j == 0,
                           pl.ds(r4 + j * A_BACK * A_STEP, A_BACK, stride=A_STEP))
        for r16 in range(A_STEP * A_STEP):
            band_block(2, q16[r16], k16[r16], v16[r16], True, pl.ds(r16, A_BACK, stride=A_STEP * A_STEP))

    n_pat = len(A_DILATIONS)
    lse = [lse_s[p] for p in range(n_pat)]
    top = functools.reduce(jnp.maximum, lse)
    wts = [jnp.exp2(x - top) for x in lse]
    total = functools.reduce(jnp.add, wts)
    out = functools.reduce(jnp.add, [w * o_s[p] for p, w in enumerate(wts)]) / total
    o_ref[...] = out.astype(BF16)


def _dilated_attn(q, k, v, *, batch, seq):
    n_tiles = seq // A_TILE
    heads, t, _ = q.shape
    blk = pl.BlockSpec((1, A_TILE, HEAD_DIM), lambda b, h, i: (h, b * n_tiles + i, 0))
    n16 = A_STEP * A_STEP
    return pl.pallas_call(
        _dilated_body,
        grid=(batch, heads, n_tiles),
        in_specs=[blk, blk, blk],
        out_specs=pl.BlockSpec((A_TILE, HEAD_DIM), lambda b, h, i: (b * n_tiles + i, h)),
        out_shape=jax.ShapeDtypeStruct((t, heads * HEAD_DIM), BF16),
        scratch_shapes=[
            pltpu.VMEM((A_TILE, HEAD_DIM), F32),
            pltpu.VMEM((A_TILE, HEAD_DIM), F32),
            pltpu.VMEM((A_STEP, A_PHASE4, HEAD_DIM), BF16),
            pltpu.VMEM((n16, A_BACK, HEAD_DIM), BF16),
        ] + [
            pltpu.VMEM((A_BACK + A_TILE, HEAD_DIM), BF16),
            pltpu.VMEM((A_STEP, A_BACK + A_PHASE4, HEAD_DIM), BF16),
            pltpu.VMEM((n16, 2 * A_BACK, HEAD_DIM), BF16),
        ] * 2 + [
            pltpu.VMEM((len(A_DILATIONS), A_TILE, HEAD_DIM), F32),
            pltpu.VMEM((len(A_DILATIONS), A_TILE, HEAD_DIM), F32),
            pltpu.VMEM((2, A_BACK, 2 * A_BACK), F32),
        ],
        compiler_params=_params("parallel", "parallel", "arbitrary"),
        name="dilated_attn",
    )(q, k, v)


def _mla_body(q_ref, k_ref, v_ref, o_ref, s_a, s_b, mx_a, mx_b, m_s, l_s, acc_s, *, tq, tk):
    qi = pl.program_id(2)
    n_tiles = tk // LANES
    n_blocks = 2 * qi + 2
    m_s[...] = jnp.full(m_s.shape, MASK_VALUE, F32)
    l_s[...] = jnp.zeros(l_s.shape, F32)
    acc_s[...] = jnp.zeros(acc_s.shape, F32)

    def key_rows(n):
        return pl.ds(pl.multiple_of(n * tk, tk), tk)

    def logits(n, s_ref, mx_ref, diag_block, r0=0):
        s = _dot_nt(q_ref[0, r0:, :], k_ref[0, key_rows(n), :])
        tiles = [s[:, c * LANES:(c + 1) * LANES] for c in range(n_tiles)]
        if diag_block is not None:
            row = lax.broadcasted_iota(jnp.int32, (tq - r0, LANES), 0) + r0
            col = lax.broadcasted_iota(jnp.int32, (tq - r0, LANES), 1) + diag_block * tk
            tiles = [jnp.where(col + c * LANES <= row, x, MASK_VALUE) for c, x in enumerate(tiles)]
        for c, x in enumerate(tiles):
            s_ref[r0:, c * LANES:(c + 1) * LANES] = x
        mx = jnp.max(functools.reduce(jnp.maximum, tiles), axis=-1, keepdims=True)
        mx_ref[r0:, :] = jnp.broadcast_to(mx, (tq - r0, LANES))

    def softmax_av(s_ref, mx_ref, n, r0=0):
        m_prev = m_s[r0:, :]
        m_new = jnp.maximum(m_prev, mx_ref[r0:, :])
        alpha = jnp.exp2(m_prev - m_new)
        p = [jnp.exp2(s_ref[r0:, c * LANES:(c + 1) * LANES] - m_new) for c in range(n_tiles)]
        l_s[r0:, :] = alpha * l_s[r0:, :] + functools.reduce(jnp.add, p)
        pb = jnp.concatenate([x.astype(BF16) for x in p], axis=1)
        acc_s[r0:, :] = alpha * acc_s[r0:, :] + _dot(pb, v_ref[0, key_rows(n), :])
        m_s[r0:, :] = m_new

    @pl.when(qi > 0)
    def _():
        logits(0, s_a, mx_a, None)

    def pair(u, carry):
        logits(2 * u + 1, s_b, mx_b, None)
        softmax_av(s_a, mx_a, 2 * u)
        logits(2 * u + 2, s_a, mx_a, None)
        softmax_av(s_b, mx_b, 2 * u + 1)
        return carry

    lax.fori_loop(0, qi - 1, pair, 0)

    @pl.when(qi > 0)
    def _():
        logits(n_blocks - 3, s_b, mx_b, None)
        softmax_av(s_a, mx_a, n_blocks - 4)
        logits(n_blocks - 2, s_a, mx_a, 0)
        softmax_av(s_b, mx_b, n_blocks - 3)

    @pl.when(qi == 0)
    def _():
        logits(0, s_a, mx_a, 0)

    logits(n_blocks - 1, s_b, mx_b, 1, r0=tk)
    softmax_av(s_a, mx_a, n_blocks - 2)
    softmax_av(s_b, mx_b, n_blocks - 1, r0=tk)
    o_ref[...] = (acc_s[...] / jnp.sum(l_s[...], axis=-1, keepdims=True)).astype(BF16)


def _mla_attn(q, k, v, *, batch, seq, tq, tk):
    assert tq == 2 * tk
    heads, t, _ = q.shape
    nq = seq // tq
    kv = pl.BlockSpec((1, seq, LANES), lambda h, b, i: (h, b, 0))
    return pl.pallas_call(
        functools.partial(_mla_body, tq=tq, tk=tk),
        grid=(heads, batch, nq),
        in_specs=[pl.BlockSpec((1, tq, LANES), lambda h, b, i: (h, b * nq + i, 0)), kv, kv],
        out_specs=pl.BlockSpec((tq, LANES), lambda h, b, i: (b * nq + i, h)),
        out_shape=jax.ShapeDtypeStruct((t, heads * LANES), BF16),
        scratch_shapes=[pltpu.VMEM((tq, tk), F32)] * 2 + [pltpu.VMEM((tq, LANES), F32)] * 5,
        compiler_params=_params("parallel", "parallel", "arbitrary"),
        name="mla_attn",
    )(q, k, v)


def _odd_proj_body(h_ref, g_ref, win_ref, pw_ref, ps_ref, oc_ref, qd_ref, kd_ref, vd_ref, ext, *, tm, tiles_per_seq):
    i = pl.program_id(0)
    xn = _rms(h_ref[...], g_ref[...]).astype(BF16)
    proj = _dot(xn, win_ref[...])

    @pl.when(i % tiles_per_seq == 0)
    def _():
        ext[0:C_HALO, :] = jnp.zeros((C_HALO, C_WIDTH), F32)

    ext[C_HALO:C_HALO + tm, :] = proj[:, 0:C_WIDTH]
    t_in_seq = (i % tiles_per_seq) * tm + lax.broadcasted_iota(jnp.int32, (tm, 1), 0)
    for g, w in enumerate(C_WINDOWS):
        cols = slice(g * C_GROUP_DIM, (g + 1) * C_GROUP_DIM)
        s = ext[C_HALO - (w - 1):C_HALO + tm, cols]
        step = 1
        while step < w:
            s = s[step:] + s[:-step]
            step *= 2
        count = jnp.minimum(t_in_seq + 1, w).astype(F32)
        pooled = s / count - proj[:, cols]
        oc_ref[:, cols] = (_dot(pooled.astype(BF16), pw_ref[g]) * ps_ref[:, cols]).astype(BF16)
    ext[0:C_HALO, :] = ext[tm:tm + C_HALO, :]

    dw = D_HEADS * HEAD_DIM
    for hd in range(D_HEADS):
        lanes = lambda base: slice(base + hd * LANES, base + (hd + 1) * LANES)
        qd_ref[hd] = (proj[:, lanes(C_WIDTH)] * (HEAD_DIM ** -0.5 * LOG2_E)).astype(BF16)
        kd_ref[hd] = proj[:, lanes(C_WIDTH + dw)].astype(BF16)
        vd_ref[hd] = proj[:, lanes(C_WIDTH + 2 * dw)].astype(BF16)


def _odd_proj(h, g, w_in, pool_w, pool_scale, *, seq, tm):
    t, d = h.shape
    row = lambda w: pl.BlockSpec((tm, w), lambda i: (i, 0))
    heads = pl.BlockSpec((D_HEADS, tm, LANES), lambda i: (0, i, 0))
    shp = jax.ShapeDtypeStruct((D_HEADS, t, LANES), BF16)
    return pl.pallas_call(
        functools.partial(_odd_proj_body, tm=tm, tiles_per_seq=seq // tm),
        grid=(t // tm,),
        in_specs=[row(d), _resident((1, d)), _resident(w_in.shape), _resident(pool_w.shape), _resident(pool_scale.shape)],
        out_specs=[row(C_WIDTH), heads, heads, heads],
        out_shape=[jax.ShapeDtypeStruct((t, C_WIDTH), BF16), shp, shp, shp],
        scratch_shapes=[pltpu.VMEM((C_HALO + tm, C_WIDTH), F32)],
        compiler_params=_params("arbitrary"),
        name="odd_proj",
    )(h, g, w_in, pool_w, pool_scale)


SB_SUB = 256
SB_DEAD_CARRY = 160.0


def _sb_body(q_ref, k_ref, v_ref, o_ref, z_a, z_b, e_a, e_b, carry_s, acc_s, *, tq, tk):
    qi = pl.program_id(2)
    n_blocks = 2 * qi + 2
    carry_s[...] = jnp.zeros(carry_s.shape, F32)
    acc_s[...] = jnp.zeros(acc_s.shape, F32)
    later = (lax.broadcasted_iota(jnp.int32, (SB_SUB, SB_SUB), 0) > lax.broadcasted_iota(jnp.int32, (SB_SUB, SB_SUB), 1))
    later = jnp.where(later, 1.0, 0.0).astype(BF16)

    def key_rows(n):
        kb = jnp.maximum(n_blocks - 1 - n, 0)
        return pl.ds(pl.multiple_of(kb * tk, tk), tk)

    def logits(n, z_ref, r0=0):
        z_ref[r0:, :] = _dot_nt(q_ref[0, r0:, :], k_ref[0, key_rows(n), :])

    def gate(z_ref, e_ref, diag_block, r0=0):
        carry = carry_s[r0:, :]
        for sub in reversed(range(tk // SB_SUB)):
            z = z_ref[r0:, sub * SB_SUB:(sub + 1) * SB_SUB]
            softplus = jnp.maximum(z, 0.0) + jnp.log2(1.0 + jnp.exp2(-jnp.abs(z)))
            arg = z - softplus
            if diag_block is None:
                keep = softplus
            else:
                row = lax.broadcasted_iota(jnp.int32, (tq - r0, SB_SUB), 0) + r0
                col = lax.broadcasted_iota(jnp.int32, (tq - r0, SB_SUB), 1) + diag_block * tk + sub * SB_SUB
                keep = jnp.where(col < row, softplus, 0.0)
                arg = jnp.where(col < row, arg, MASK_VALUE)
            expo = arg - _dot(keep.astype(BF16), later)
            for c in range(SB_SUB // LANES):
                lo = sub * SB_SUB + c * LANES
                e_ref[r0:, lo:lo + LANES] = expo[:, c * LANES:(c + 1) * LANES] - carry
            carry = carry + jnp.sum(keep, axis=-1, keepdims=True)
        carry_s[r0:, :] = carry

    def weigh(e_ref, n, r0=0):
        acc_s[r0:, :] += _dot(jnp.exp2(e_ref[r0:, :]).astype(BF16), v_ref[0, key_rows(n), :])

    logits(0, z_a, r0=tk)
    logits(1, z_b)
    gate(z_a, e_a, 1, r0=tk)
    logits(2, z_a)
    gate(z_b, e_b, 0)
    weigh(e_a, 0, r0=tk)

    def step(n, z_this, e_this, z_next, e_prev):
        logits(n + 1, z_next)
        gate(z_this, e_this, None)
        weigh(e_prev, n - 1)

    def one_block(state):
        n, _ = state

        @pl.when(n % 2 == 0)
        def _():
            step(n, z_a, e_a, z_b, e_b)

        @pl.when(n % 2 == 1)
        def _():
            step(n, z_b, e_b, z_a, e_a)

        return n + 1, jnp.min(carry_s[...])

    def more_blocks(state):
        n, min_carry = state
        return (n < n_blocks) & (min_carry < SB_DEAD_CARRY)

    n_done, _ = lax.while_loop(more_blocks, one_block, (jnp.int32(2), jnp.min(carry_s[...])))

    @pl.when(n_done % 2 == 0)
    def _():
        weigh(e_b, n_done - 1)

    @pl.when(n_done % 2 == 1)
    def _():
        weigh(e_a, n_done - 1)

    o_ref[...] = acc_s[...].astype(BF16)


def _sb_attn(q, k, v, *, batch, seq, tq, tk):
    assert tq == 2 * tk and tk % SB_SUB == 0
    heads, t, _ = q.shape
    nq = seq // tq
    kv = pl.BlockSpec((1, seq, LANES), lambda h, b, i: (h, b, 0))
    return pl.pallas_call(
        functools.partial(_sb_body, tq=tq, tk=tk),
        grid=(heads, batch, nq),
        in_specs=[pl.BlockSpec((1, tq, LANES), lambda h, b, i: (h, b * nq + i, 0)), kv, kv],
        out_specs=pl.BlockSpec((tq, LANES), lambda h, b, i: (b * nq + i, h)),
        out_shape=jax.ShapeDtypeStruct((t, heads * LANES), BF16),
        scratch_shapes=[pltpu.VMEM((tq, tk), F32)] * 4 + [pltpu.VMEM((tq, LANES), F32)] * 2,
        compiler_params=_params("parallel", "parallel", "arbitrary"),
        name="sb_attn",
    )(q, k, v)


def _latent_head_lanes():
    half_pe, half_nope = B_ROPE // 2, B_NOPE // 2
    pe = np.concatenate([np.arange(half_pe), LANES // 2 + np.arange(half_pe)])
    nope = np.concatenate([half_pe + np.arange(half_nope), LANES // 2 + half_pe + np.arange(half_nope)])
    return pe, nope


def _scatter_cols(w, src_cols, dst_cols, width):
    out = jnp.zeros((w.shape[0], width), w.dtype)
    return out.at[:, np.asarray(dst_cols)].set(w[:, np.asarray(src_cols)])


def _even_weights(w_in, w_q_up, w_kv_up):
    pe, nope = _latent_head_lanes()
    a_w = A_HEADS * HEAD_DIM
    o_kr = 3 * a_w + B_Q_LORA + B_KV_LORA
    k_rope = _scatter_cols(w_in, o_kr + np.arange(B_ROPE), pe, LANES)
    w_in_p = jnp.concatenate([w_in[:, :o_kr], k_rope], axis=1).astype(BF16)
    qd, kvd = B_NOPE + B_ROPE, B_NOPE + B_V
    src_q, dst_q, src_k, dst_k, src_v = [], [], [], [], []
    for hd in range(B_HEADS):
        src_q += list(hd * qd + np.arange(B_NOPE)) + list(hd * qd + B_NOPE + np.arange(B_ROPE))
        dst_q += list(hd * LANES + nope) + list(hd * LANES + pe)
        src_k += list(hd * kvd + np.arange(B_NOPE))
        dst_k += list(hd * LANES + nope)
        src_v += list(hd * kvd + B_NOPE + np.arange(B_V))
    w_q = _scatter_cols(w_q_up, src_q, dst_q, B_HEADS * LANES).astype(BF16)
    w_k = _scatter_cols(w_kv_up, src_k, dst_k, B_HEADS * LANES)
    w_kv = jnp.concatenate([w_k, w_kv_up[:, np.asarray(src_v)]], axis=1).astype(BF16)
    return w_in_p, w_q, w_kv


def _rope_frequencies():
    inv_a = ROPE_THETA ** (-jnp.arange(0, HEAD_DIM, 2, dtype=F32) / HEAD_DIM)
    inv_b = ROPE_THETA ** (-jnp.arange(0, B_ROPE, 2, dtype=F32) / B_ROPE)
    pad = jnp.zeros((LANES - inv_a.shape[0] - inv_b.shape[0],), F32)
    return jnp.concatenate([inv_a, inv_b, pad])[None, :]


def kernel(x, positions, norm_g, ffn_w_gate, ffn_w_up, ffn_w_down, even_w_in, even_q_norm, even_w_q_up, even_kv_norm, even_w_kv_up, even_w_out, odd_w_in, odd_pool_w, odd_pool_scale, odd_w_out, final_norm):
    batch, seq, d = x.shape
    t = batch * seq
    assert seq % A_TILE == 0
    tm = 512
    h = x.reshape(t, d).astype(F32)
    pos = positions.reshape(t, 1).astype(jnp.int32)
    bf = lambda w: w.astype(BF16)
    gain = lambda i, j: norm_g[i, j][None, :].astype(F32)
    wg, wu, wd = bf(ffn_w_gate), bf(ffn_w_up), bf(ffn_w_down)
    ffn = functools.partial(_ffn, wg=wg, wu=wu, wd=wd, tm=tm)

    h = ffn(h, [], None, gain(0, 0), which=(0, 0))
    w_in_p, w_q, w_kv = _even_weights(even_w_in[0], even_w_q_up[0], even_w_kv_up[0])
    qa, ka, va, qb, kb, vb = _even_proj(h, gain(0, 1), w_in_p, pos, _rope_frequencies(), even_q_norm[0][None, :].astype(F32), w_q,
                                        even_kv_norm[0][None, :].astype(F32), w_kv, tm=tm)
    out_a = _dilated_attn(qa, ka, va, batch=batch, seq=seq)
    out_b = _mla_attn(qb, kb, vb, batch=batch, seq=seq, tq=1024, tk=512)
    h = ffn(h, [out_a, out_b], bf(even_w_out[0]), gain(0, 2), which=(0, 1))

    h = ffn(h, [], None, gain(1, 0), which=(1, 0))
    out_c, qd, kd, vd = _odd_proj(h, gain(1, 1), bf(odd_w_in[0]), bf(odd_pool_w[0]),
                                  odd_pool_scale[0][None, :].astype(F32), seq=seq, tm=tm)
    out_d = _sb_attn(qd, kd, vd, batch=batch, seq=seq, tq=512, tk=256)
    h = ffn(h, [out_c, out_d], bf(odd_w_out[0]), gain(1, 2), which=(1, 1), final_g=final_norm[None, :].astype(F32))
    return h.reshape(batch, seq, d).astype(x.dtype)
```

```python
import functools

import jax
import jax.numpy as jnp
import numpy as np
from jax import lax
from jax.experimental import pallas as pl
from jax.experimental.pallas import tpu as pltpu

F32 = jnp.float32
BF16 = jnp.bfloat16

HEAD_DIM = 128
LANES = 128
ROPE_THETA = 10000.0
NORM_EPS = 1e-6
A_HEADS = 6
A_DILATIONS = (1, 4, 16)
A_BACK = 128
B_HEADS = 4
B_Q_LORA = 384
B_KV_LORA = 256
B_NOPE = 64
B_ROPE = 32
B_V = 128
C_WINDOWS = (2, 4, 8, 16)
C_GROUP_DIM = 128
C_WIDTH = len(C_WINDOWS) * C_GROUP_DIM
C_HALO = 16
D_HEADS = 4
MASK_VALUE = -1e30
LOG2_E = 1.4426950408889634
VMEM_LIMIT_BYTES = 56 * 1024 * 1024


def _params(*semantics):
    return pltpu.CompilerParams(dimension_semantics=semantics, vmem_limit_bytes=VMEM_LIMIT_BYTES)


def _resident(shape):
    return pl.BlockSpec(shape, lambda *_: (0,) * len(shape), pipeline_mode=pl.Buffered(1))


def _rms(x, g):
    ms = jnp.mean(x * x, axis=-1, keepdims=True)
    return x * lax.rsqrt(ms + NORM_EPS) * g


def _dot(a, b):
    return jnp.dot(a, b, preferred_element_type=F32)


def _dot_nt(a, b):
    return lax.dot_general(a, b, (((1,), (1,)), ((), ())), preferred_element_type=F32)


def _ffn_body(*refs, mix_widths, final_norm):
    h_ref = refs[0]
    n_mix = len(mix_widths)
    mix_refs = refs[1:1 + n_mix]
    base = 1 + n_mix
    wo_ref = refs[base] if n_mix else None
    base += 1 if n_mix else 0
    g_ref, wg_ref, wu_ref, wd_ref = refs[base:base + 4]
    fg_ref = refs[base + 4] if final_norm else None
    o_ref = refs[-1]

    h = h_ref[...]
    row = 0
    for m_ref, width in zip(mix_refs, mix_widths):
        h = h + _dot(m_ref[...], wo_ref[row:row + width, :])
        row += width
    xn = _rms(h, g_ref[...]).astype(BF16)
    gate = _dot(xn, wg_ref[...])
    up = _dot(xn, wu_ref[...])
    act = (gate * jax.nn.sigmoid(gate) * up).astype(BF16)
    y = h + 0.5 * _dot(act, wd_ref[...])
    if final_norm:
        y = _rms(y, fg_ref[...])
    o_ref[...] = y


def _ffn(h, mix, w_out, g, wg, wu, wd, which, final_g=None, *, tm):
    t, d = h.shape
    d_ff = wg.shape[-1]
    row = lambda w: pl.BlockSpec((tm, w), lambda i: (i, 0))
    pick = lambda r, c: pl.BlockSpec((None, None, r, c), lambda i: (*which, 0, 0), pipeline_mode=pl.Buffered(1))
    args, specs = [h], [row(d)]
    for m in mix:
        args.append(m)
        specs.append(row(m.shape[1]))
    if mix:
        args.append(w_out)
        specs.append(_resident(w_out.shape))
    args += [g, wg, wu, wd]
    specs += [_resident((1, d)), pick(d, d_ff), pick(d, d_ff), pick(d_ff, d)]
    if final_g is not None:
        args.append(final_g)
        specs.append(_resident((1, d)))
    return pl.pallas_call(
        functools.partial(_ffn_body, mix_widths=tuple(m.shape[1] for m in mix), final_norm=final_g is not None),
        grid=(t // tm,),
        in_specs=specs,
        out_specs=row(d),
        out_shape=jax.ShapeDtypeStruct((t, d), F32),
        compiler_params=_params("parallel"),
        name="ffn",
    )(*args)


def _rope_tables(pos, inv):
    lane = lax.broadcasted_iota(jnp.int32, (1, LANES), 1)
    half = LANES // 2
    sign = jnp.where(lane < half, -1.0, 1.0).astype(F32)
    ang = pos * inv
    c, s = jnp.cos(ang), jnp.sin(ang)
    c_sw, s_sw = pltpu.roll(c, half, 1), pltpu.roll(s, half, 1)
    first, second = lane < B_ROPE // 2, (lane >= half) & (lane < half + B_ROPE // 2)
    return (jnp.where(lane < half, c, c_sw), sign * jnp.where(lane < half, s, s_sw),
            jnp.where(first, c_sw, jnp.where(second, c, 1.0)), sign * jnp.where(first, s_sw, jnp.where(second, s, 0.0)))


def _rotate(x, c, s):
    return x * c + pltpu.roll(x, LANES // 2, 1) * s


def _even_proj_body(h_ref, g_ref, win_ref, pos_ref, inv_ref, qn_ref, wq_ref, kvn_ref, wkv_ref,
                    qa_ref, ka_ref, va_ref, qb_ref, kb_ref, vb_ref):
    xn = _rms(h_ref[...], g_ref[...]).astype(BF16)
    proj = _dot(xn, win_ref[...])
    ca, sa, cb, sb = _rope_tables(pos_ref[...].astype(F32), inv_ref[...])
    a_w = A_HEADS * HEAD_DIM
    blk = lambda base, i: slice(base + i * LANES, base + (i + 1) * LANES)
    for hd in range(A_HEADS):
        qa_ref[hd] = (_rotate(proj[:, blk(0, hd)], ca, sa) * (HEAD_DIM ** -0.5 * LOG2_E)).astype(BF16)
        ka_ref[hd] = _rotate(proj[:, blk(a_w, hd)], ca, sa).astype(BF16)
        va_ref[hd] = proj[:, blk(2 * a_w, hd)].astype(BF16)
    o_cq = 3 * a_w
    o_ckv = o_cq + B_Q_LORA
    o_kr = o_ckv + B_KV_LORA
    cq = _rms(proj[:, o_cq:o_ckv], qn_ref[...]).astype(BF16)
    ckv = _rms(proj[:, o_ckv:o_kr], kvn_ref[...]).astype(BF16)
    qb = _dot(cq, wq_ref[...])
    kv = _dot(ckv, wkv_ref[...])
    k_pe = _rotate(proj[:, o_kr:o_kr + LANES], cb, sb)
    q_scale = (B_NOPE + B_ROPE) ** -0.5 * LOG2_E
    for hd in range(B_HEADS):
        qb_ref[hd] = (_rotate(qb[:, blk(0, hd)], cb, sb) * q_scale).astype(BF16)
        kb_ref[hd] = (kv[:, blk(0, hd)] + k_pe).astype(BF16)
        vb_ref[hd] = kv[:, blk(B_HEADS * LANES, hd)].astype(BF16)


def _even_proj(h, g, w_in, pos, inv, q_norm, w_q, kv_norm, w_kv, *, tm):
    t, d = h.shape
    row = lambda w: pl.BlockSpec((tm, w), lambda i: (i, 0))
    heads = lambda n: pl.BlockSpec((n, tm, LANES), lambda i: (0, i, 0))
    shp = lambda n: jax.ShapeDtypeStruct((n, t, LANES), BF16)
    return pl.pallas_call(
        _even_proj_body,
        grid=(t // tm,),
        in_specs=[row(d), _resident((1, d)), _resident(w_in.shape), row(1), _resident(inv.shape),
                  _resident(q_norm.shape), _resident(w_q.shape), _resident(kv_norm.shape), _resident(w_kv.shape)],
        out_specs=[heads(A_HEADS)] * 3 + [heads(B_HEADS)] * 3,
        out_shape=[shp(A_HEADS)] * 3 + [shp(B_HEADS)] * 3,
        compiler_params=_params("parallel"),
        name="even_proj",
    )(h, g, w_in, pos, inv, q_norm, w_q, kv_norm, w_kv)


A_TILE = A_BACK * max(A_DILATIONS)
A_STEP = 4
assert A_DILATIONS == (1, A_STEP, A_STEP * A_STEP)
A_PHASE4 = A_TILE // A_STEP


def _dilated_body(q_ref, k_ref, v_ref, o_ref, nat, p4f, q4, q16, k1, k4, k16, v1, v4, v16, o_s, lse_s, bias):
    i = pl.program_id(2)

    @pl.when(i > 0)
    def _():
        for buf in (k1, v1):
            buf[0:A_BACK] = buf[A_TILE:A_TILE + A_BACK]
        for buf in (k4, v4):
            buf[:, 0:A_BACK] = buf[:, A_PHASE4:A_PHASE4 + A_BACK]
        for buf in (k16, v16):
            buf[:, 0:A_BACK] = buf[:, A_BACK:2 * A_BACK]

    @pl.when(i == 0)
    def _():
        for buf in (k1, v1):
            buf[0:A_BACK] = jnp.zeros((A_BACK, HEAD_DIM), BF16)
        for buf in (k4, v4, k16, v16):
            buf[:, 0:A_BACK] = jnp.zeros((buf.shape[0], A_BACK, HEAD_DIM), BF16)

    def regroup(x_ref, d4, d16, halo):
        nat[...] = x_ref[0].astype(F32)
        for r4 in range(A_STEP):
            rows = nat[pl.ds(r4, A_PHASE4, stride=A_STEP), :]
            p4f[r4 * A_PHASE4:(r4 + 1) * A_PHASE4] = rows
            d4[r4, halo:halo + A_PHASE4] = rows.astype(BF16)
        for r16 in range(A_STEP * A_STEP):
            r4, sub = r16 % A_STEP, r16 // A_STEP
            rows = p4f[pl.ds(r4 * A_PHASE4 + sub, A_BACK, stride=A_STEP), :]
            d16[r16, halo:halo + A_BACK] = rows.astype(BF16)

    regroup(q_ref, q4, q16, 0)
    k1[A_BACK:A_BACK + A_TILE] = k_ref[0]
    regroup(k_ref, k4, k16, A_BACK)
    v1[A_BACK:A_BACK + A_TILE] = v_ref[0]
    regroup(v_ref, v4, v16, A_BACK)

    qi = lax.broadcasted_iota(jnp.int32, (A_BACK, 2 * A_BACK), 0)
    ki = lax.broadcasted_iota(jnp.int32, (A_BACK, 2 * A_BACK), 1)
    band = (ki >= qi) & (ki <= qi + A_BACK)
    bias[0] = jnp.where(band, 0.0, MASK_VALUE)
    bias[1] = jnp.where(band & (ki >= jnp.where(i > 0, 0, A_BACK)), 0.0, MASK_VALUE)

    def band_block(p, qb, kb, vb, first, out_rows):
        s = _dot_nt(qb, kb) + bias[1 if first else 0]
        m = jnp.max(s, axis=-1, keepdims=True)
        e = jnp.exp2(s - m)
        den = jnp.sum(e, axis=-1, keepdims=True)
        o_s[p, out_rows, :] = _dot(e.astype(BF16), vb) / den
        lse_s[p, out_rows, :] = jnp.broadcast_to(m + jnp.log2(den), (A_BACK, HEAD_DIM))

    @pl.when(i >= 0)
    def _():
        for j in range(A_TILE // A_BACK):
            blk, bnd = slice(j * A_BACK, (j + 1) * A_BACK), slice(j * A_BACK, (j + 2) * A_BACK)
            band_block(0, q_ref[0, blk, :], k1[bnd], v1[bnd], j == 0, blk)
        for r4 in range(A_STEP):
            for j in range(A_PHASE4 // A_BACK):
                blk, bnd = slice(j * A_BACK, (j + 1) * A_BACK), slice(j * A_BACK, (j + 2) * A_BACK)
                band_block(1, q4[r4, blk], k4[r4, bnd], v4[r4, bnd], j == 0,
                           pl.ds(r4 + j * A_BACK * A_STEP, A_BACK, stride=A_STEP))
        for r16 in range(A_STEP * A_STEP):
            band_block(2, q16[r16], k16[r16], v16[r16], True, pl.ds(r16, A_BACK, stride=A_STEP * A_STEP))

    n_pat = len(A_DILATIONS)
    lse = [lse_s[p] for p in range(n_pat)]
    top = functools.reduce(jnp.maximum, lse)
    wts = [jnp.exp2(x - top) for x in lse]
    total = functools.reduce(jnp.add, wts)
    out = functools.reduce(jnp.add, [w * o_s[p] for p, w in enumerate(wts)]) / total
    o_ref[...] = out.astype(BF16)


def _dilated_attn(q, k, v, *, batch, seq):
    n_tiles = seq // A_TILE
    heads, t, _ = q.shape
    blk = pl.BlockSpec((1, A_TILE, HEAD_DIM), lambda b, h, i: (h, b * n_tiles + i, 0))
    n16 = A_STEP * A_STEP
    return pl.pallas_call(
        _dilated_body,
        grid=(batch, heads, n_tiles),
        in_specs=[blk, blk, blk],
        out_specs=pl.BlockSpec((A_TILE, HEAD_DIM), lambda b, h, i: (b * n_tiles + i, h)),
        out_shape=jax.ShapeDtypeStruct((t, heads * HEAD_DIM), BF16),
        scratch_shapes=[
            pltpu.VMEM((A_TILE, HEAD_DIM), F32),
            pltpu.VMEM((A_TILE, HEAD_DIM), F32),
            pltpu.VMEM((A_STEP, A_PHASE4, HEAD_DIM), BF16),
            pltpu.VMEM((n16, A_BACK, HEAD_DIM), BF16),
        ] + [
            pltpu.VMEM((A_BACK + A_TILE, HEAD_DIM), BF16),
            pltpu.VMEM((A_STEP, A_BACK + A_PHASE4, HEAD_DIM), BF16),
            pltpu.VMEM((n16, 2 * A_BACK, HEAD_DIM), BF16),
        ] * 2 + [
            pltpu.VMEM((len(A_DILATIONS), A_TILE, HEAD_DIM), F32),
            pltpu.VMEM((len(A_DILATIONS), A_TILE, HEAD_DIM), F32),
            pltpu.VMEM((2, A_BACK, 2 * A_BACK), F32),
        ],
        compiler_params=_params("parallel", "parallel", "arbitrary"),
        name="dilated_attn",
    )(q, k, v)


def _mla_body(q_ref, k_ref, v_ref, o_ref, s_a, s_b, mx_a, mx_b, al_s, m_s, l_s, acc_s, *, tq, tk):
    qi = pl.program_id(2)
    n_tiles = tk // LANES
    n_blocks = 2 * qi + 2
    m_s[...] = jnp.full(m_s.shape, MASK_VALUE, F32)
    l_s[...] = jnp.zeros(l_s.shape, F32)
    acc_s[...] = jnp.zeros(acc_s.shape, F32)

    def key_rows(n):
        return pl.ds(pl.multiple_of(n * tk, tk), tk)

    def logits(n, s_ref, mx_ref, diag_block, r0=0):
        s = _dot_nt(q_ref[0, r0:, :], k_ref[0, key_rows(n), :])
        tiles = [s[:, c * LANES:(c + 1) * LANES] for c in range(n_tiles)]
        if diag_block is not None:
            row = lax.broadcasted_iota(jnp.int32, (tq - r0, LANES), 0) + r0
            col = lax.broadcasted_iota(jnp.int32, (tq - r0, LANES), 1) + diag_block * tk
            tiles = [jnp.where(col + c * LANES <= row, x, MASK_VALUE) for c, x in enumerate(tiles)]
        for c, x in enumerate(tiles):
            s_ref[r0:, c * LANES:(c + 1) * LANES] = x
        mx = jnp.max(functools.reduce(jnp.maximum, tiles), axis=-1, keepdims=True)
        mx_ref[r0:, :] = jnp.broadcast_to(mx, (tq - r0, LANES))

    def softmax_av(s_ref, mx_ref, n, r0=0):
        m_prev = m_s[r0:, :]
        m_new = jnp.maximum(m_prev, mx_ref[r0:, :])
        m_s[r0:, :] = m_new
        al_s[r0:, :] = jnp.exp2(m_prev - m_new)
        p = [jnp.exp2(s_ref[r0:, c * LANES:(c + 1) * LANES] - m_s[r0:, :]) for c in range(n_tiles)]
        l_s[r0:, :] = al_s[r0:, :] * l_s[r0:, :] + functools.reduce(jnp.add, p)
        pb = jnp.concatenate([x.astype(BF16) for x in p], axis=1)
        acc_s[r0:, :] = al_s[r0:, :] * acc_s[r0:, :] + _dot(pb, v_ref[0, key_rows(n), :])

    @pl.when(qi > 0)
    def _():
        logits(0, s_a, mx_a, None)

    def pair(u, carry):
        logits(2 * u + 1, s_b, mx_b, None)
        softmax_av(s_a, mx_a, 2 * u)
        logits(2 * u + 2, s_a, mx_a, None)
        softmax_av(s_b, mx_b, 2 * u + 1)
        return carry

    lax.fori_loop(0, qi - 1, pair, 0)

    @pl.when(qi > 0)
    def _():
        logits(n_blocks - 3, s_b, mx_b, None)
        softmax_av(s_a, mx_a, n_blocks - 4)
        logits(n_blocks - 2, s_a, mx_a, 0)
        softmax_av(s_b, mx_b, n_blocks - 3)

    @pl.when(qi == 0)
    def _():
        logits(0, s_a, mx_a, 0)

    logits(n_blocks - 1, s_b, mx_b, 1, r0=tk)
    softmax_av(s_a, mx_a, n_blocks - 2)
    softmax_av(s_b, mx_b, n_blocks - 1, r0=tk)
    o_ref[...] = (acc_s[...] / jnp.sum(l_s[...], axis=-1, keepdims=True)).astype(BF16)


def _mla_attn(q, k, v, *, batch, seq, tq, tk):
    assert tq == 2 * tk
    heads, t, _ = q.shape
    nq = seq // tq
    kv = pl.BlockSpec((1, seq, LANES), lambda h, b, i: (h, b, 0))
    return pl.pallas_call(
        functools.partial(_mla_body, tq=tq, tk=tk),
        grid=(heads, batch, nq),
        in_specs=[pl.BlockSpec((1, tq, LANES), lambda h, b, i: (h, b * nq + i, 0)), kv, kv],
        out_specs=pl.BlockSpec((tq, LANES), lambda h, b, i: (b * nq + i, h)),
        out_shape=jax.ShapeDtypeStruct((t, heads * LANES), BF16),
        scratch_shapes=[pltpu.VMEM((tq, tk), F32)] * 2 + [pltpu.VMEM((tq, LANES), F32)] * 6,
        compiler_params=_params("parallel", "parallel", "arbitrary"),
        name="mla_attn",
    )(q, k, v)


def _odd_proj_body(h_ref, g_ref, win_ref, pw_ref, ps_ref, oc_ref, qd_ref, kd_ref, vd_ref, ext, *, tm, tiles_per_seq):
    i = pl.program_id(0)
    xn = _rms(h_ref[...], g_ref[...]).astype(BF16)
    proj = _dot(xn, win_ref[...])

    @pl.when(i % tiles_per_seq == 0)
    def _():
        ext[0:C_HALO, :] = jnp.zeros((C_HALO, C_WIDTH), F32)

    ext[C_HALO:C_HALO + tm, :] = proj[:, 0:C_WIDTH]
    t_in_seq = (i % tiles_per_seq) * tm + lax.broadcasted_iota(jnp.int32, (tm, 1), 0)
    for g, w in enumerate(C_WINDOWS):
        cols = slice(g * C_GROUP_DIM, (g + 1) * C_GROUP_DIM)
        s = ext[C_HALO - (w - 1):C_HALO + tm, cols]
        step = 1
        while step < w:
            s = s[step:] + s[:-step]
            step *= 2
        count = jnp.minimum(t_in_seq + 1, w).astype(F32)
        pooled = s / count - proj[:, cols]
        oc_ref[:, cols] = (_dot(pooled.astype(BF16), pw_ref[g]) * ps_ref[:, cols]).astype(BF16)
    ext[0:C_HALO, :] = ext[tm:tm + C_HALO, :]

    dw = D_HEADS * HEAD_DIM
    for hd in range(D_HEADS):
        lanes = lambda base: slice(base + hd * LANES, base + (hd + 1) * LANES)
        qd_ref[hd] = (proj[:, lanes(C_WIDTH)] * (HEAD_DIM ** -0.5 * LOG2_E)).astype(BF16)
        kd_ref[hd] = proj[:, lanes(C_WIDTH + dw)].astype(BF16)
        vd_ref[hd] = proj[:, lanes(C_WIDTH + 2 * dw)].astype(BF16)


def _odd_proj(h, g, w_in, pool_w, pool_scale, *, seq, tm):
    t, d = h.shape
    row = lambda w: pl.BlockSpec((tm, w), lambda i: (i, 0))
    heads = pl.BlockSpec((D_HEADS, tm, LANES), lambda i: (0, i, 0))
    shp = jax.ShapeDtypeStruct((D_HEADS, t, LANES), BF16)
    return pl.pallas_call(
        functools.partial(_odd_proj_body, tm=tm, tiles_per_seq=seq // tm),
        grid=(t // tm,),
        in_specs=[row(d), _resident((1, d)), _resident(w_in.shape), _resident(pool_w.shape), _resident(pool_scale.shape)],
        out_specs=[row(C_WIDTH), heads, heads, heads],
        out_shape=[jax.ShapeDtypeStruct((t, C_WIDTH), BF16), shp, shp, shp],
        scratch_shapes=[pltpu.VMEM((C_HALO + tm, C_WIDTH), F32)],
        compiler_params=_params("arbitrary"),
        name="odd_proj",
    )(h, g, w_in, pool_w, pool_scale)


SB_SUB = 256
SB_DEAD_CARRY = 160.0


def _sb_body(q_ref, k_ref, v_ref, o_ref, z_a, z_b, e_a, e_b, carry_s, acc_s, *, tq, tk, heads):
    qi = pl.program_id(2)
    n_blocks = 2 * qi + 2
    carry_s[...] = jnp.zeros(carry_s.shape, F32)
    acc_s[...] = jnp.zeros(acc_s.shape, F32)
    later = (lax.broadcasted_iota(jnp.int32, (SB_SUB, SB_SUB), 0) > lax.broadcasted_iota(jnp.int32, (SB_SUB, SB_SUB), 1))
    later = jnp.where(later, 1.0, 0.0).astype(BF16)

    def key_rows(n):
        kb = jnp.maximum(n_blocks - 1 - n, 0)
        return pl.ds(pl.multiple_of(kb * tk, tk), tk)

    def logits(n, z_ref, r0=0):
        for h in range(heads):
            z_ref[h, r0:, :] = _dot_nt(q_ref[h, r0:, :], k_ref[h, key_rows(n), :])

    def gate(z_ref, e_ref, diag_block, r0=0):
        for h in range(heads):
            carry = carry_s[h, r0:, :]
            for sub in reversed(range(tk // SB_SUB)):
                z = z_ref[h, r0:, sub * SB_SUB:(sub + 1) * SB_SUB]
                softplus = jnp.maximum(z, 0.0) + jnp.log2(1.0 + jnp.exp2(-jnp.abs(z)))
                arg = z - softplus
                if diag_block is None:
                    keep = softplus
                else:
                    row = lax.broadcasted_iota(jnp.int32, (tq - r0, SB_SUB), 0) + r0
                    col = lax.broadcasted_iota(jnp.int32, (tq - r0, SB_SUB), 1) + diag_block * tk + sub * SB_SUB
                    keep = jnp.where(col < row, softplus, 0.0)
                    arg = jnp.where(col < row, arg, MASK_VALUE)
                expo = arg - _dot(keep.astype(BF16), later)
                for c in range(SB_SUB // LANES):
                    lo = sub * SB_SUB + c * LANES
                    e_ref[h, r0:, lo:lo + LANES] = expo[:, c * LANES:(c + 1) * LANES] - carry
                carry = carry + jnp.sum(keep, axis=-1, keepdims=True)
            carry_s[h, r0:, :] = carry

    def weigh(e_ref, n, r0=0):
        for h in range(heads):
            acc_s[h, r0:, :] += _dot(jnp.exp2(e_ref[h, r0:, :]).astype(BF16), v_ref[h, key_rows(n), :])

    logits(0, z_a, r0=tk)
    logits(1, z_b)
    gate(z_a, e_a, 1, r0=tk)
    logits(2, z_a)
    gate(z_b, e_b, 0)
    weigh(e_a, 0, r0=tk)

    def step(n, z_this, e_this, z_next, e_prev):
        logits(n + 1, z_next)
        gate(z_this, e_this, None)
        weigh(e_prev, n - 1)

    def one_block(state):
        n, _ = state

        @pl.when(n % 2 == 0)
        def _():
            step(n, z_a, e_a, z_b, e_b)

        @pl.when(n % 2 == 1)
        def _():
            step(n, z_b, e_b, z_a, e_a)

        return n + 1, jnp.min(carry_s[...])

    def more_blocks(state):
        n, min_carry = state
        return (n < n_blocks) & (min_carry < SB_DEAD_CARRY)

    n_done, _ = lax.while_loop(more_blocks, one_block, (jnp.int32(2), jnp.min(carry_s[...])))

    @pl.when(n_done % 2 == 0)
    def _():
        weigh(e_b, n_done - 1)

    @pl.when(n_done % 2 == 1)
    def _():
        weigh(e_a, n_done - 1)

    for h in range(heads):
        o_ref[:, h * LANES:(h + 1) * LANES] = acc_s[h].astype(BF16)


def _sb_attn(q, k, v, *, batch, seq, tq, tk, heads_per_step):
    assert tq == 2 * tk and tk % SB_SUB == 0
    heads, t, _ = q.shape
    hp = heads_per_step
    nq = seq // tq
    kv = pl.BlockSpec((hp, seq, LANES), lambda h, b, i: (h, b, 0), pipeline_mode=pl.Buffered(1))
    return pl.pallas_call(
        functools.partial(_sb_body, tq=tq, tk=tk, heads=hp),
        grid=(heads // hp, batch, nq),
        in_specs=[pl.BlockSpec((hp, tq, LANES), lambda h, b, i: (h, b * nq + i, 0)), kv, kv],
        out_specs=pl.BlockSpec((tq, hp * LANES), lambda h, b, i: (b * nq + i, h)),
        out_shape=jax.ShapeDtypeStruct((t, heads * LANES), BF16),
        scratch_shapes=[pltpu.VMEM((hp, tq, tk), F32)] * 4 + [pltpu.VMEM((hp, tq, LANES), F32)] * 2,
        compiler_params=_params("parallel", "parallel", "arbitrary"),
        name="sb_attn",
    )(q, k, v)


def _latent_head_lanes():
    half_pe, half_nope = B_ROPE // 2, B_NOPE // 2
    pe = np.concatenate([np.arange(half_pe), LANES // 2 + np.arange(half_pe)])
    nope = np.concatenate([half_pe + np.arange(half_nope), LANES // 2 + half_pe + np.arange(half_nope)])
    return pe, nope


def _scatter_cols(w, src_cols, dst_cols, width):
    out = jnp.zeros((w.shape[0], width), w.dtype)
    return out.at[:, np.asarray(dst_cols)].set(w[:, np.asarray(src_cols)])


def _even_weights(w_in, w_q_up, w_kv_up):
    pe, nope = _latent_head_lanes()
    a_w = A_HEADS * HEAD_DIM
    o_kr = 3 * a_w + B_Q_LORA + B_KV_LORA
    k_rope = _scatter_cols(w_in, o_kr + np.arange(B_ROPE), pe, LANES)
    w_in_p = jnp.concatenate([w_in[:, :o_kr], k_rope], axis=1).astype(BF16)
    qd, kvd = B_NOPE + B_ROPE, B_NOPE + B_V
    src_q, dst_q, src_k, dst_k, src_v = [], [], [], [], []
    for hd in range(B_HEADS):
        src_q += list(hd * qd + np.arange(B_NOPE)) + list(hd * qd + B_NOPE + np.arange(B_ROPE))
        dst_q += list(hd * LANES + nope) + list(hd * LANES + pe)
        src_k += list(hd * kvd + np.arange(B_NOPE))
        dst_k += list(hd * LANES + nope)
        src_v += list(hd * kvd + B_NOPE + np.arange(B_V))
    w_q = _scatter_cols(w_q_up, src_q, dst_q, B_HEADS * LANES).astype(BF16)
    w_k = _scatter_cols(w_kv_up, src_k, dst_k, B_HEADS * LANES)
    w_kv = jnp.concatenate([w_k, w_kv_up[:, np.asarray(src_v)]], axis=1).astype(BF16)
    return w_in_p, w_q, w_kv


def _rope_frequencies():
    inv_a = ROPE_THETA ** (-jnp.arange(0, HEAD_DIM, 2, dtype=F32) / HEAD_DIM)
    inv_b = ROPE_THETA ** (-jnp.arange(0, B_ROPE, 2, dtype=F32) / B_ROPE)
    pad = jnp.zeros((LANES - inv_a.shape[0] - inv_b.shape[0],), F32)
    return jnp.concatenate([inv_a, inv_b, pad])[None, :]


def kernel(x, positions, norm_g, ffn_w_gate, ffn_w_up, ffn_w_down, even_w_in, even_q_norm, even_w_q_up, even_kv_norm, even_w_kv_up, even_w_out, odd_w_in, odd_pool_w, odd_pool_scale, odd_w_out, final_norm):
    batch, seq, d = x.shape
    t = batch * seq
    assert seq % A_TILE == 0
    tm = 512
    h = x.reshape(t, d).astype(F32)
    pos = positions.reshape(t, 1).astype(jnp.int32)
    bf = lambda w: w.astype(BF16)
    gain = lambda i, j: norm_g[i, j][None, :].astype(F32)
    wg, wu, wd = bf(ffn_w_gate), bf(ffn_w_up), bf(ffn_w_down)
    ffn = functools.partial(_ffn, wg=wg, wu=wu, wd=wd, tm=tm)

    h = ffn(h, [], None, gain(0, 0), which=(0, 0))
    w_in_p, w_q, w_kv = _even_weights(even_w_in[0], even_w_q_up[0], even_w_kv_up[0])
    qa, ka, va, qb, kb, vb = _even_proj(h, gain(0, 1), w_in_p, pos, _rope_frequencies(), even_q_norm[0][None, :].astype(F32), w_q,
                                        even_kv_norm[0][None, :].astype(F32), w_kv, tm=tm)
    out_a = _dilated_attn(qa, ka, va, batch=batch, seq=seq)
    out_b = _mla_attn(qb, kb, vb, batch=batch, seq=seq, tq=1024, tk=512)
    h = ffn(h, [out_a, out_b], bf(even_w_out[0]), gain(0, 2), which=(0, 1))

    h = ffn(h, [], None, gain(1, 0), which=(1, 0))
    out_c, qd, kd, vd = _odd_proj(h, gain(1, 1), bf(odd_w_in[0]), bf(odd_pool_w[0]),
                                  odd_pool_scale[0][None, :].astype(F32), seq=seq, tm=tm)
    out_d = _sb_attn(qd, kd, vd, batch=batch, seq=seq, tq=512, tk=256, heads_per_step=4)
    h = ffn(h, [out_c, out_d], bf(odd_w_out[0]), gain(1, 2), which=(1, 1), final_g=final_norm[None, :].astype(F32))
    return h.reshape(batch, seq, d).astype(x.dtype)
```

```python
import functools

import jax
import jax.numpy as jnp
import numpy as np
from jax import lax
from jax.experimental import pallas as pl
from jax.experimental.pallas import tpu as pltpu

F32 = jnp.float32
BF16 = jnp.bfloat16

HEAD_DIM = 128
LANES = 128
ROPE_THETA = 10000.0
NORM_EPS = 1e-6
A_HEADS = 6
A_DILATIONS = (1, 4, 16)
A_BACK = 128
B_HEADS = 4
B_Q_LORA = 384
B_KV_LORA = 256
B_NOPE = 64
B_ROPE = 32
B_V = 128
C_WINDOWS = (2, 4, 8, 16)
C_GROUP_DIM = 128
C_WIDTH = len(C_WINDOWS) * C_GROUP_DIM
C_HALO = 16
D_HEADS = 4
MASK_VALUE = -1e30
LOG2_E = 1.4426950408889634
VMEM_LIMIT_BYTES = 56 * 1024 * 1024


def _params(*semantics):
    return pltpu.CompilerParams(dimension_semantics=semantics, vmem_limit_bytes=VMEM_LIMIT_BYTES)


def _resident(shape):
    return pl.BlockSpec(shape, lambda *_: (0,) * len(shape), pipeline_mode=pl.Buffered(1))


def _rms(x, g):
    ms = jnp.mean(x * x, axis=-1, keepdims=True)
    return x * lax.rsqrt(ms + NORM_EPS) * g


def _dot(a, b):
    return jnp.dot(a, b, preferred_element_type=F32)


def _dot_nt(a, b):
    return lax.dot_general(a, b, (((1,), (1,)), ((), ())), preferred_element_type=F32)


def _ffn_body(*refs, mix_widths, final_norm):
    h_ref = refs[0]
    n_mix = len(mix_widths)
    mix_refs = refs[1:1 + n_mix]
    base = 1 + n_mix
    wo_ref = refs[base] if n_mix else None
    base += 1 if n_mix else 0
    g_ref, wg_ref, wu_ref, wd_ref = refs[base:base + 4]
    fg_ref = refs[base + 4] if final_norm else None
    o_ref = refs[-1]

    h = h_ref[...]
    row = 0
    for m_ref, width in zip(mix_refs, mix_widths):
        h = h + _dot(m_ref[...], wo_ref[row:row + width, :])
        row += width
    xn = _rms(h, g_ref[...]).astype(BF16)
    gate = _dot(xn, wg_ref[...])
    up = _dot(xn, wu_ref[...])
    act = (gate * jax.nn.sigmoid(gate) * up).astype(BF16)
    y = h + 0.5 * _dot(act, wd_ref[...])
    if final_norm:
        y = _rms(y, fg_ref[...])
    o_ref[...] = y


def _ffn(h, mix, w_out, g, wg, wu, wd, which, final_g=None, *, tm):
    t, d = h.shape
    d_ff = wg.shape[-1]
    row = lambda w: pl.BlockSpec((tm, w), lambda i: (i, 0))
    pick = lambda r, c: pl.BlockSpec((None, None, r, c), lambda i: (*which, 0, 0), pipeline_mode=pl.Buffered(1))
    args, specs = [h], [row(d)]
    for m in mix:
        args.append(m)
        specs.append(row(m.shape[1]))
    if mix:
        args.append(w_out)
        specs.append(_resident(w_out.shape))
    args += [g, wg, wu, wd]
    specs += [_resident((1, d)), pick(d, d_ff), pick(d, d_ff), pick(d_ff, d)]
    if final_g is not None:
        args.append(final_g)
        specs.append(_resident((1, d)))
    return pl.pallas_call(
        functools.partial(_ffn_body, mix_widths=tuple(m.shape[1] for m in mix), final_norm=final_g is not None),
        grid=(t // tm,),
        in_specs=specs,
        out_specs=row(d),
        out_shape=jax.ShapeDtypeStruct((t, d), F32),
        compiler_params=_params("parallel"),
        name="ffn",
    )(*args)


def _rope_tables(pos, inv):
    lane = lax.broadcasted_iota(jnp.int32, (1, LANES), 1)
    half = LANES // 2
    sign = jnp.where(lane < half, -1.0, 1.0).astype(F32)
    ang = pos * inv
    c, s = jnp.cos(ang), jnp.sin(ang)
    c_sw, s_sw = pltpu.roll(c, half, 1), pltpu.roll(s, half, 1)
    first, second = lane < B_ROPE // 2, (lane >= half) & (lane < half + B_ROPE // 2)
    return (jnp.where(lane < half, c, c_sw), sign * jnp.where(lane < half, s, s_sw),
            jnp.where(first, c_sw, jnp.where(second, c, 1.0)), sign * jnp.where(first, s_sw, jnp.where(second, s, 0.0)))


def _rotate(x, c, s):
    return x * c + pltpu.roll(x, LANES // 2, 1) * s


def _even_proj_body(h_ref, g_ref, win_ref, pos_ref, inv_ref, qn_ref, wq_ref, kvn_ref, wkv_ref,
                    qa_ref, ka_ref, va_ref, qb_ref, kb_ref, vb_ref):
    xn = _rms(h_ref[...], g_ref[...]).astype(BF16)
    proj = _dot(xn, win_ref[...])
    ca, sa, cb, sb = _rope_tables(pos_ref[...].astype(F32), inv_ref[...])
    a_w = A_HEADS * HEAD_DIM
    blk = lambda base, i: slice(base + i * LANES, base + (i + 1) * LANES)
    for hd in range(A_HEADS):
        qa_ref[hd] = (_rotate(proj[:, blk(0, hd)], ca, sa) * (HEAD_DIM ** -0.5 * LOG2_E)).astype(BF16)
        ka_ref[hd] = _rotate(proj[:, blk(a_w, hd)], ca, sa).astype(BF16)
        va_ref[hd] = proj[:, blk(2 * a_w, hd)].astype(BF16)
    o_cq = 3 * a_w
    o_ckv = o_cq + B_Q_LORA
    o_kr = o_ckv + B_KV_LORA
    cq = _rms(proj[:, o_cq:o_ckv], qn_ref[...]).astype(BF16)
    ckv = _rms(proj[:, o_ckv:o_kr], kvn_ref[...]).astype(BF16)
    qb = _dot(cq, wq_ref[...])
    kv = _dot(ckv, wkv_ref[...])
    k_pe = _rotate(proj[:, o_kr:o_kr + LANES], cb, sb)
    q_scale = (B_NOPE + B_ROPE) ** -0.5 * LOG2_E
    for hd in range(B_HEADS):
        qb_ref[hd] = (_rotate(qb[:, blk(0, hd)], cb, sb) * q_scale).astype(BF16)
        kb_ref[hd] = (kv[:, blk(0, hd)] + k_pe).astype(BF16)
        vb_ref[hd] = kv[:, blk(B_HEADS * LANES, hd)].astype(BF16)


def _even_proj(h, g, w_in, pos, inv, q_norm, w_q, kv_norm, w_kv, *, tm):
    t, d = h.shape
    row = lambda w: pl.BlockSpec((tm, w), lambda i: (i, 0))
    heads = lambda n: pl.BlockSpec((n, tm, LANES), lambda i: (0, i, 0))
    shp = lambda n: jax.ShapeDtypeStruct((n, t, LANES), BF16)
    return pl.pallas_call(
        _even_proj_body,
        grid=(t // tm,),
        in_specs=[row(d), _resident((1, d)), _resident(w_in.shape), row(1), _resident(inv.shape),
                  _resident(q_norm.shape), _resident(w_q.shape), _resident(kv_norm.shape), _resident(w_kv.shape)],
        out_specs=[heads(A_HEADS)] * 3 + [heads(B_HEADS)] * 3,
        out_shape=[shp(A_HEADS)] * 3 + [shp(B_HEADS)] * 3,
        compiler_params=_params("parallel"),
        name="even_proj",
    )(h, g, w_in, pos, inv, q_norm, w_q, kv_norm, w_kv)


A_TILE = A_BACK * max(A_DILATIONS)
A_STEP = 4
assert A_DILATIONS == (1, A_STEP, A_STEP * A_STEP)
A_PHASE4 = A_TILE // A_STEP


def _dilated_body(q_ref, k_ref, v_ref, o_ref, nat, p4f, q4, q16, k1, k4, k16, v1, v4, v16, o_s, lse_s, bias):
    i = pl.program_id(2)

    @pl.when(i > 0)
    def _():
        for buf in (k1, v1):
            buf[0:A_BACK] = buf[A_TILE:A_TILE + A_BACK]
        for buf in (k4, v4):
            buf[:, 0:A_BACK] = buf[:, A_PHASE4:A_PHASE4 + A_BACK]
        for buf in (k16, v16):
            buf[:, 0:A_BACK] = buf[:, A_BACK:2 * A_BACK]

    @pl.when(i == 0)
    def _():
        for buf in (k1, v1):
            buf[0:A_BACK] = jnp.zeros((A_BACK, HEAD_DIM), BF16)
        for buf in (k4, v4, k16, v16):
            buf[:, 0:A_BACK] = jnp.zeros((buf.shape[0], A_BACK, HEAD_DIM), BF16)

    def regroup(x_ref, d4, d16, halo):
        nat[...] = x_ref[0].astype(F32)
        for r4 in range(A_STEP):
            rows = nat[pl.ds(r4, A_PHASE4, stride=A_STEP), :]
            p4f[r4 * A_PHASE4:(r4 + 1) * A_PHASE4] = rows
            d4[r4, halo:halo + A_PHASE4] = rows.astype(BF16)
        for r16 in range(A_STEP * A_STEP):
            r4, sub = r16 % A_STEP, r16 // A_STEP
            rows = p4f[pl.ds(r4 * A_PHASE4 + sub, A_BACK, stride=A_STEP), :]
            d16[r16, halo:halo + A_BACK] = rows.astype(BF16)

    regroup(q_ref, q4, q16, 0)
    k1[A_BACK:A_BACK + A_TILE] = k_ref[0]
    regroup(k_ref, k4, k16, A_BACK)
    v1[A_BACK:A_BACK + A_TILE] = v_ref[0]
    regroup(v_ref, v4, v16, A_BACK)

    qi = lax.broadcasted_iota(jnp.int32, (A_BACK, 2 * A_BACK), 0)
    ki = lax.broadcasted_iota(jnp.int32, (A_BACK, 2 * A_BACK), 1)
    band = (ki >= qi) & (ki <= qi + A_BACK)
    bias[0] = jnp.where(band, 0.0, MASK_VALUE)
    bias[1] = jnp.where(band & (ki >= jnp.where(i > 0, 0, A_BACK)), 0.0, MASK_VALUE)

    def band_block(p, qb, kb, vb, first, out_rows):
        s = _dot_nt(qb, kb) + bias[1 if first else 0]
        m = jnp.max(s, axis=-1, keepdims=True)
        e = jnp.exp2(s - m).astype(BF16)
        res = _dot(e, jnp.concatenate([vb, jnp.ones_like(vb)], axis=1))
        den = res[:, HEAD_DIM:]
        o_s[p, out_rows, :] = res[:, :HEAD_DIM] / den
        lse_s[p, out_rows, :] = m + jnp.log2(den)

    @pl.when(i >= 0)
    def _():
        for j in range(A_TILE // A_BACK):
            blk, bnd = slice(j * A_BACK, (j + 1) * A_BACK), slice(j * A_BACK, (j + 2) * A_BACK)
            band_block(0, q_ref[0, blk, :], k1[bnd], v1[bnd], j == 0, blk)
        for r4 in range(A_STEP):
            for j in range(A_PHASE4 // A_BACK):
                blk, bnd = slice(j * A_BACK, (j + 1) * A_BACK), slice(j * A_BACK, (j + 2) * A_BACK)
                band_block(1, q4[r4, blk], k4[r4, bnd], v4[r4, bnd], j == 0,
                           pl.ds(r4 + j * A_BACK * A_STEP, A_BACK, stride=A_STEP))
        for r16 in range(A_STEP * A_STEP):
            band_block(2, q16[r16], k16[r16], v16[r16], True, pl.ds(r16, A_BACK, stride=A_STEP * A_STEP))

    n_pat = len(A_DILATIONS)
    lse = [lse_s[p] for p in range(n_pat)]
    top = functools.reduce(jnp.maximum, lse)
    wts = [jnp.exp2(x - top) for x in lse]
    total = functools.reduce(jnp.add, wts)
    out = functools.reduce(jnp.add, [w * o_s[p] for p, w in enumerate(wts)]) / total
    o_ref[...] = out.astype(BF16)


def _dilated_attn(q, k, v, *, batch, seq):
    n_tiles = seq // A_TILE
    heads, t, _ = q.shape
    blk = pl.BlockSpec((1, A_TILE, HEAD_DIM), lambda b, h, i: (h, b * n_tiles + i, 0))
    n16 = A_STEP * A_STEP
    return pl.pallas_call(
        _dilated_body,
        grid=(batch, heads, n_tiles),
        in_specs=[blk, blk, blk],
        out_specs=pl.BlockSpec((A_TILE, HEAD_DIM), lambda b, h, i: (b * n_tiles + i, h)),
        out_shape=jax.ShapeDtypeStruct((t, heads * HEAD_DIM), BF16),
        scratch_shapes=[
            pltpu.VMEM((A_TILE, HEAD_DIM), F32),
            pltpu.VMEM((A_TILE, HEAD_DIM), F32),
            pltpu.VMEM((A_STEP, A_PHASE4, HEAD_DIM), BF16),
            pltpu.VMEM((n16, A_BACK, HEAD_DIM), BF16),
        ] + [
            pltpu.VMEM((A_BACK + A_TILE, HEAD_DIM), BF16),
            pltpu.VMEM((A_STEP, A_BACK + A_PHASE4, HEAD_DIM), BF16),
            pltpu.VMEM((n16, 2 * A_BACK, HEAD_DIM), BF16),
        ] * 2 + [
            pltpu.VMEM((len(A_DILATIONS), A_TILE, HEAD_DIM), F32),
            pltpu.VMEM((len(A_DILATIONS), A_TILE, HEAD_DIM), F32),
            pltpu.VMEM((2, A_BACK, 2 * A_BACK), F32),
        ],
        compiler_params=_params("parallel", "parallel", "arbitrary"),
        name="dilated_attn",
    )(q, k, v)


def _mla_body(q_ref, k_ref, v_ref, o_ref, s_a, s_b, mx_a, mx_b, al_s, m_s, l_s, acc_s, *, tq, tk):
    qi = pl.program_id(2)
    n_tiles = tk // LANES
    n_blocks = 2 * qi + 2
    m_s[...] = jnp.full(m_s.shape, MASK_VALUE, F32)
    l_s[...] = jnp.zeros(l_s.shape, F32)
    acc_s[...] = jnp.zeros(acc_s.shape, F32)

    def key_rows(n):
        return pl.ds(pl.multiple_of(n * tk, tk), tk)

    def logits(n, s_ref, mx_ref, diag_block, r0=0):
        s = _dot_nt(q_ref[0, r0:, :], k_ref[0, key_rows(n), :])
        tiles = [s[:, c * LANES:(c + 1) * LANES] for c in range(n_tiles)]
        if diag_block is not None:
            row = lax.broadcasted_iota(jnp.int32, (tq - r0, LANES), 0) + r0
            col = lax.broadcasted_iota(jnp.int32, (tq - r0, LANES), 1) + diag_block * tk
            tiles = [jnp.where(col + c * LANES <= row, x, MASK_VALUE) for c, x in enumerate(tiles)]
        for c, x in enumerate(tiles):
            s_ref[r0:, c * LANES:(c + 1) * LANES] = x
        mx = jnp.max(functools.reduce(jnp.maximum, tiles), axis=-1, keepdims=True)
        mx_ref[r0:, :] = jnp.broadcast_to(mx, (tq - r0, LANES))

    def softmax_av(s_ref, mx_ref, n, r0=0):
        m_prev = m_s[r0:, :]
        m_new = jnp.maximum(m_prev, mx_ref[r0:, :])
        m_s[r0:, :] = m_new
        al_s[r0:, :] = jnp.exp2(m_prev - m_new)
        pb = jnp.concatenate([jnp.exp2(s_ref[r0:, c * LANES:(c + 1) * LANES] - m_s[r0:, :]).astype(BF16)
                              for c in range(n_tiles)], axis=1)
        v = v_ref[0, key_rows(n), :]
        res = _dot(pb, jnp.concatenate([v, jnp.ones_like(v)], axis=1))
        l_s[r0:, :] = al_s[r0:, :] * l_s[r0:, :] + res[:, LANES:]
        acc_s[r0:, :] = al_s[r0:, :] * acc_s[r0:, :] + res[:, :LANES]

    @pl.when(qi > 0)
    def _():
        logits(0, s_a, mx_a, None)

    def pair(u, carry):
        logits(2 * u + 1, s_b, mx_b, None)
        softmax_av(s_a, mx_a, 2 * u)
        logits(2 * u + 2, s_a, mx_a, None)
        softmax_av(s_b, mx_b, 2 * u + 1)
        return carry

    lax.fori_loop(0, qi - 1, pair, 0)

    @pl.when(qi > 0)
    def _():
        logits(n_blocks - 3, s_b, mx_b, None)
        softmax_av(s_a, mx_a, n_blocks - 4)
        logits(n_blocks - 2, s_a, mx_a, 0)
        softmax_av(s_b, mx_b, n_blocks - 3)

    @pl.when(qi == 0)
    def _():
        logits(0, s_a, mx_a, 0)

    logits(n_blocks - 1, s_b, mx_b, 1, r0=tk)
    softmax_av(s_a, mx_a, n_blocks - 2)
    softmax_av(s_b, mx_b, n_blocks - 1, r0=tk)
    o_ref[...] = (acc_s[...] / l_s[...]).astype(BF16)


def _mla_attn(q, k, v, *, batch, seq, tq, tk):
    assert tq == 2 * tk
    heads, t, _ = q.shape
    nq = seq // tq
    kv = pl.BlockSpec((1, seq, LANES), lambda h, b, i: (h, b, 0))
    return pl.pallas_call(
        functools.partial(_mla_body, tq=tq, tk=tk),
        grid=(heads, batch, nq),
        in_specs=[pl.BlockSpec((1, tq, LANES), lambda h, b, i: (h, b * nq + i, 0)), kv, kv],
        out_specs=pl.BlockSpec((tq, LANES), lambda h, b, i: (b * nq + i, h)),
        out_shape=jax.ShapeDtypeStruct((t, heads * LANES), BF16),
        scratch_shapes=[pltpu.VMEM((tq, tk), F32)] * 2 + [pltpu.VMEM((tq, LANES), F32)] * 6,
        compiler_params=_params("parallel", "parallel", "arbitrary"),
        name="mla_attn",
    )(q, k, v)


def _odd_proj_body(h_ref, g_ref, win_ref, pw_ref, ps_ref, oc_ref, qd_ref, kd_ref, vd_ref, ext, *, tm, tiles_per_seq):
    i = pl.program_id(0)
    xn = _rms(h_ref[...], g_ref[...]).astype(BF16)
    proj = _dot(xn, win_ref[...])

    @pl.when(i % tiles_per_seq == 0)
    def _():
        ext[0:C_HALO, :] = jnp.zeros((C_HALO, C_WIDTH), F32)

    ext[C_HALO:C_HALO + tm, :] = proj[:, 0:C_WIDTH]
    t_in_seq = (i % tiles_per_seq) * tm + lax.broadcasted_iota(jnp.int32, (tm, 1), 0)
    for g, w in enumerate(C_WINDOWS):
        cols = slice(g * C_GROUP_DIM, (g + 1) * C_GROUP_DIM)
        s = ext[C_HALO - (w - 1):C_HALO + tm, cols]
        step = 1
        while step < w:
            s = s[step:] + s[:-step]
            step *= 2
        count = jnp.minimum(t_in_seq + 1, w).astype(F32)
        pooled = s / count - proj[:, cols]
        oc_ref[:, cols] = (_dot(pooled.astype(BF16), pw_ref[g]) * ps_ref[:, cols]).astype(BF16)
    ext[0:C_HALO, :] = ext[tm:tm + C_HALO, :]

    dw = D_HEADS * HEAD_DIM
    for hd in range(D_HEADS):
        lanes = lambda base: slice(base + hd * LANES, base + (hd + 1) * LANES)
        qd_ref[hd] = (proj[:, lanes(C_WIDTH)] * (HEAD_DIM ** -0.5 * LOG2_E)).astype(BF16)
        kd_ref[hd] = proj[:, lanes(C_WIDTH + dw)].astype(BF16)
        vd_ref[hd] = proj[:, lanes(C_WIDTH + 2 * dw)].astype(BF16)


def _odd_proj(h, g, w_in, pool_w, pool_scale, *, seq, tm):
    t, d = h.shape
    row = lambda w: pl.BlockSpec((tm, w), lambda i: (i, 0))
    heads = pl.BlockSpec((D_HEADS, tm, LANES), lambda i: (0, i, 0))
    shp = jax.ShapeDtypeStruct((D_HEADS, t, LANES), BF16)
    return pl.pallas_call(
        functools.partial(_odd_proj_body, tm=tm, tiles_per_seq=seq // tm),
        grid=(t // tm,),
        in_specs=[row(d), _resident((1, d)), _resident(w_in.shape), _resident(pool_w.shape), _resident(pool_scale.shape)],
        out_specs=[row(C_WIDTH), heads, heads, heads],
        out_shape=[jax.ShapeDtypeStruct((t, C_WIDTH), BF16), shp, shp, shp],
        scratch_shapes=[pltpu.VMEM((C_HALO + tm, C_WIDTH), F32)],
        compiler_params=_params("arbitrary"),
        name="odd_proj",
    )(h, g, w_in, pool_w, pool_scale)


SB_SUB = 256
SB_DEAD_CARRY = 160.0


def _sb_body(q_ref, k_ref, v_ref, o_ref, z_a, z_b, e_a, e_b, carry_s, acc_s, *, tq, tk, heads):
    qi = pl.program_id(2)
    n_blocks = 2 * qi + 2
    carry_s[...] = jnp.zeros(carry_s.shape, F32)
    acc_s[...] = jnp.zeros(acc_s.shape, F32)
    later = (lax.broadcasted_iota(jnp.int32, (SB_SUB, SB_SUB), 0) > lax.broadcasted_iota(jnp.int32, (SB_SUB, SB_SUB), 1))
    later = jnp.where(later, 1.0, 0.0).astype(BF16)

    def key_rows(n):
        kb = jnp.maximum(n_blocks - 1 - n, 0)
        return pl.ds(pl.multiple_of(kb * tk, tk), tk)

    def logits(n, z_ref, r0=0):
        for h in range(heads):
            z_ref[h, r0:, :] = _dot_nt(q_ref[h, r0:, :], k_ref[h, key_rows(n), :])

    def gate(z_ref, e_ref, diag_block, r0=0):
        for h in range(heads):
            carry = carry_s[h, r0:, :]
            for sub in reversed(range(tk // SB_SUB)):
                z = z_ref[h, r0:, sub * SB_SUB:(sub + 1) * SB_SUB]
                softplus = jnp.maximum(z, 0.0) + jnp.log2(1.0 + jnp.exp2(-jnp.abs(z)))
                arg = z - softplus
                if diag_block is None:
                    keep = softplus
                else:
                    row = lax.broadcasted_iota(jnp.int32, (tq - r0, SB_SUB), 0) + r0
                    col = lax.broadcasted_iota(jnp.int32, (tq - r0, SB_SUB), 1) + diag_block * tk + sub * SB_SUB
                    keep = jnp.where(col < row, softplus, 0.0)
                    arg = jnp.where(col < row, arg, MASK_VALUE)
                expo = arg - _dot(keep.astype(BF16), later)
                for c in range(SB_SUB // LANES):
                    lo = sub * SB_SUB + c * LANES
                    e_ref[h, r0:, lo:lo + LANES] = expo[:, c * LANES:(c + 1) * LANES] - carry
                carry = carry + jnp.sum(keep, axis=-1, keepdims=True)
            carry_s[h, r0:, :] = carry

    def weigh(e_ref, n, r0=0):
        for h in range(heads):
            acc_s[h, r0:, :] += _dot(jnp.exp2(e_ref[h, r0:, :]).astype(BF16), v_ref[h, key_rows(n), :])

    logits(0, z_a, r0=tk)
    logits(1, z_b)
    gate(z_a, e_a, 1, r0=tk)
    logits(2, z_a)
    gate(z_b, e_b, 0)
    weigh(e_a, 0, r0=tk)

    def step(n, z_this, e_this, z_next, e_prev):
        logits(n + 1, z_next)
        gate(z_this, e_this, None)
        weigh(e_prev, n - 1)

    def one_block(state):
        n, _ = state

        @pl.when(n % 2 == 0)
        def _():
            step(n, z_a, e_a, z_b, e_b)

        @pl.when(n % 2 == 1)
        def _():
            step(n, z_b, e_b, z_a, e_a)

        return n + 1, jnp.min(carry_s[...])

    def more_blocks(state):
        n, min_carry = state
        return (n < n_blocks) & (min_carry < SB_DEAD_CARRY)

    n_done, _ = lax.while_loop(more_blocks, one_block, (jnp.int32(2), jnp.min(carry_s[...])))

    @pl.when(n_done % 2 == 0)
    def _():
        weigh(e_b, n_done - 1)

    @pl.when(n_done % 2 == 1)
    def _():
        weigh(e_a, n_done - 1)

    for h in range(heads):
        o_ref[:, h * LANES:(h + 1) * LANES] = acc_s[h].astype(BF16)


def _sb_attn(q, k, v, *, batch, seq, tq, tk, heads_per_step):
    assert tq == 2 * tk and tk % SB_SUB == 0
    heads, t, _ = q.shape
    hp = heads_per_step
    nq = seq // tq
    kv = pl.BlockSpec((hp, seq, LANES), lambda h, b, i: (h, b, 0), pipeline_mode=pl.Buffered(1))
    return pl.pallas_call(
        functools.partial(_sb_body, tq=tq, tk=tk, heads=hp),
        grid=(heads // hp, batch, nq),
        in_specs=[pl.BlockSpec((hp, tq, LANES), lambda h, b, i: (h, b * nq + i, 0)), kv, kv],
        out_specs=pl.BlockSpec((tq, hp * LANES), lambda h, b, i: (b * nq + i, h)),
        out_shape=jax.ShapeDtypeStruct((t, heads * LANES), BF16),
        scratch_shapes=[pltpu.VMEM((hp, tq, tk), F32)] * 4 + [pltpu.VMEM((hp, tq, LANES), F32)] * 2,
        compiler_params=_params("parallel", "parallel", "arbitrary"),
        name="sb_attn",
    )(q, k, v)


def _latent_head_lanes():
    half_pe, half_nope = B_ROPE // 2, B_NOPE // 2
    pe = np.concatenate([np.arange(half_pe), LANES // 2 + np.arange(half_pe)])
    nope = np.concatenate([half_pe + np.arange(half_nope), LANES // 2 + half_pe + np.arange(half_nope)])
    return pe, nope


def _scatter_cols(w, src_cols, dst_cols, width):
    out = jnp.zeros((w.shape[0], width), w.dtype)
    return out.at[:, np.asarray(dst_cols)].set(w[:, np.asarray(src_cols)])


def _even_weights(w_in, w_q_up, w_kv_up):
    pe, nope = _latent_head_lanes()
    a_w = A_HEADS * HEAD_DIM
    o_kr = 3 * a_w + B_Q_LORA + B_KV_LORA
    k_rope = _scatter_cols(w_in, o_kr + np.arange(B_ROPE), pe, LANES)
    w_in_p = jnp.concatenate([w_in[:, :o_kr], k_rope], axis=1).astype(BF16)
    qd, kvd = B_NOPE + B_ROPE, B_NOPE + B_V
    src_q, dst_q, src_k, dst_k, src_v = [], [], [], [], []
    for hd in range(B_HEADS):
        src_q += list(hd * qd + np.arange(B_NOPE)) + list(hd * qd + B_NOPE + np.arange(B_ROPE))
        dst_q += list(hd * LANES + nope) + list(hd * LANES + pe)
        src_k += list(hd * kvd + np.arange(B_NOPE))
        dst_k += list(hd * LANES + nope)
        src_v += list(hd * kvd + B_NOPE + np.arange(B_V))
    w_q = _scatter_cols(w_q_up, src_q, dst_q, B_HEADS * LANES).astype(BF16)
    w_k = _scatter_cols(w_kv_up, src_k, dst_k, B_HEADS * LANES)
    w_kv = jnp.concatenate([w_k, w_kv_up[:, np.asarray(src_v)]], axis=1).astype(BF16)
    return w_in_p, w_q, w_kv


def _rope_frequencies():
    inv_a = ROPE_THETA ** (-jnp.arange(0, HEAD_DIM, 2, dtype=F32) / HEAD_DIM)
    inv_b = ROPE_THETA ** (-jnp.arange(0, B_ROPE, 2, dtype=F32) / B_ROPE)
    pad = jnp.zeros((LANES - inv_a.shape[0] - inv_b.shape[0],), F32)
    return jnp.concatenate([inv_a, inv_b, pad])[None, :]


def kernel(x, positions, norm_g, ffn_w_gate, ffn_w_up, ffn_w_down, even_w_in, even_q_norm, even_w_q_up, even_kv_norm, even_w_kv_up, even_w_out, odd_w_in, odd_pool_w, odd_pool_scale, odd_w_out, final_norm):
    batch, seq, d = x.shape
    t = batch * seq
    assert seq % A_TILE == 0
    tm = 512
    h = x.reshape(t, d).astype(F32)
    pos = positions.reshape(t, 1).astype(jnp.int32)
    bf = lambda w: w.astype(BF16)
    gain = lambda i, j: norm_g[i, j][None, :].astype(F32)
    wg, wu, wd = bf(ffn_w_gate), bf(ffn_w_up), bf(ffn_w_down)
    ffn = functools.partial(_ffn, wg=wg, wu=wu, wd=wd, tm=tm)

    h = ffn(h, [], None, gain(0, 0), which=(0, 0))
    w_in_p, w_q, w_kv = _even_weights(even_w_in[0], even_w_q_up[0], even_w_kv_up[0])
    qa, ka, va, qb, kb, vb = _even_proj(h, gain(0, 1), w_in_p, pos, _rope_frequencies(), even_q_norm[0][None, :].astype(F32), w_q,
                                        even_kv_norm[0][None, :].astype(F32), w_kv, tm=tm)
    out_a = _dilated_attn(qa, ka, va, batch=batch, seq=seq)
    out_b = _mla_attn(qb, kb, vb, batch=batch, seq=seq, tq=1024, tk=512)
    h = ffn(h, [out_a, out_b], bf(even_w_out[0]), gain(0, 2), which=(0, 1))

    h = ffn(h, [], None, gain(1, 0), which=(1, 0))
    out_c, qd, kd, vd = _odd_proj(h, gain(1, 1), bf(odd_w_in[0]), bf(odd_pool_w[0]),
                                  odd_pool_scale[0][None, :].astype(F32), seq=seq, tm=tm)
    out_d = _sb_attn(qd, kd, vd, batch=batch, seq=seq, tq=512, tk=256, heads_per_step=4)
    h = ffn(h, [out_c, out_d], bf(odd_w_out[0]), gain(1, 2), which=(1, 1), final_g=final_norm[None, :].astype(F32))
    return h.reshape(batch, seq, d).astype(x.dtype)
```

```python
import functools

import jax
import jax.numpy as jnp
import numpy as np
from jax import lax
from jax.experimental import pallas as pl
from jax.experimental.pallas import tpu as pltpu

F32 = jnp.float32
BF16 = jnp.bfloat16

HEAD_DIM = 128
LANES = 128
ROPE_THETA = 10000.0
NORM_EPS = 1e-6
A_HEADS = 6
A_DILATIONS = (1, 4, 16)
A_BACK = 128
B_HEADS = 4
B_Q_LORA = 384
B_KV_LORA = 256
B_NOPE = 64
B_ROPE = 32
B_V = 128
C_WINDOWS = (2, 4, 8, 16)
C_GROUP_DIM = 128
C_WIDTH = len(C_WINDOWS) * C_GROUP_DIM
C_HALO = 16
D_HEADS = 4
MASK_VALUE = -1e30
LOG2_E = 1.4426950408889634
VMEM_LIMIT_BYTES = 56 * 1024 * 1024


def _params(*semantics):
    return pltpu.CompilerParams(dimension_semantics=semantics, vmem_limit_bytes=VMEM_LIMIT_BYTES)


def _resident(shape):
    return pl.BlockSpec(shape, lambda *_: (0,) * len(shape), pipeline_mode=pl.Buffered(1))


def _rms(x, g):
    ms = jnp.mean(x * x, axis=-1, keepdims=True)
    return x * lax.rsqrt(ms + NORM_EPS) * g


def _dot(a, b):
    return jnp.dot(a, b, preferred_element_type=F32)


def _rms_dot(x, g, *weights):
    r = lax.rsqrt(jnp.mean(x * x, axis=-1, keepdims=True) + NORM_EPS)
    xg = (x * g).astype(BF16)
    return [r * _dot(xg, w) for w in weights]


def _dot_nt(a, b):
    return lax.dot_general(a, b, (((1,), (1,)), ((), ())), preferred_element_type=F32)


def _ffn_body(*refs, mix_widths, final_norm):
    h_ref = refs[0]
    n_mix = len(mix_widths)
    mix_refs = refs[1:1 + n_mix]
    base = 1 + n_mix
    wo_ref = refs[base] if n_mix else None
    base += 1 if n_mix else 0
    g_ref, wg_ref, wu_ref, wd_ref = refs[base:base + 4]
    fg_ref = refs[base + 4] if final_norm else None
    o_ref = refs[-1]

    h = h_ref[...]
    row = 0
    for m_ref, width in zip(mix_refs, mix_widths):
        h = h + _dot(m_ref[...], wo_ref[row:row + width, :])
        row += width
    gate, up = _rms_dot(h, g_ref[...], wg_ref[...], wu_ref[...])
    act = (gate * jax.nn.sigmoid(gate) * up).astype(BF16)
    y = h + 0.5 * _dot(act, wd_ref[...])
    if final_norm:
        y = _rms(y, fg_ref[...])
    o_ref[...] = y


def _ffn(h, mix, w_out, g, wg, wu, wd, which, final_g=None, *, tm):
    t, d = h.shape
    d_ff = wg.shape[-1]
    row = lambda w: pl.BlockSpec((tm, w), lambda i: (i, 0))
    pick = lambda r, c: pl.BlockSpec((None, None, r, c), lambda i: (*which, 0, 0), pipeline_mode=pl.Buffered(1))
    args, specs = [h], [row(d)]
    for m in mix:
        args.append(m)
        specs.append(row(m.shape[1]))
    if mix:
        args.append(w_out)
        specs.append(_resident(w_out.shape))
    args += [g, wg, wu, wd]
    specs += [_resident((1, d)), pick(d, d_ff), pick(d, d_ff), pick(d_ff, d)]
    if final_g is not None:
        args.append(final_g)
        specs.append(_resident((1, d)))
    return pl.pallas_call(
        functools.partial(_ffn_body, mix_widths=tuple(m.shape[1] for m in mix), final_norm=final_g is not None),
        grid=(t // tm,),
        in_specs=specs,
        out_specs=row(d),
        out_shape=jax.ShapeDtypeStruct((t, d), F32),
        compiler_params=_params("parallel"),
        name="ffn",
    )(*args)


def _rope_tables(pos, inv):
    lane = lax.broadcasted_iota(jnp.int32, (1, LANES), 1)
    half = LANES // 2
    sign = jnp.where(lane < half, -1.0, 1.0).astype(F32)
    ang = pos * inv
    c, s = jnp.cos(ang), jnp.sin(ang)
    c_sw, s_sw = pltpu.roll(c, half, 1), pltpu.roll(s, half, 1)
    first, second = lane < B_ROPE // 2, (lane >= half) & (lane < half + B_ROPE // 2)
    return (jnp.where(lane < half, c, c_sw), sign * jnp.where(lane < half, s, s_sw),
            jnp.where(first, c_sw, jnp.where(second, c, 1.0)), sign * jnp.where(first, s_sw, jnp.where(second, s, 0.0)))


def _rotate(x, c, s):
    return x * c + pltpu.roll(x, LANES // 2, 1) * s


def _even_proj_body(h_ref, g_ref, win_ref, pos_ref, inv_ref, qn_ref, wq_ref, kvn_ref, wkv_ref,
                    qa_ref, ka_ref, va_ref, qb_ref, kb_ref, vb_ref):
    xn = _rms(h_ref[...], g_ref[...]).astype(BF16)
    proj = _dot(xn, win_ref[...])
    ca, sa, cb, sb = _rope_tables(pos_ref[...].astype(F32), inv_ref[...])
    a_w = A_HEADS * HEAD_DIM
    blk = lambda base, i: slice(base + i * LANES, base + (i + 1) * LANES)
    for hd in range(A_HEADS):
        qa_ref[hd] = (_rotate(proj[:, blk(0, hd)], ca, sa) * (HEAD_DIM ** -0.5 * LOG2_E)).astype(BF16)
        ka_ref[hd] = _rotate(proj[:, blk(a_w, hd)], ca, sa).astype(BF16)
        va_ref[hd] = proj[:, blk(2 * a_w, hd)].astype(BF16)
    o_cq = 3 * a_w
    o_ckv = o_cq + B_Q_LORA
    o_kr = o_ckv + B_KV_LORA
    cq = _rms(proj[:, o_cq:o_ckv], qn_ref[...]).astype(BF16)
    ckv = _rms(proj[:, o_ckv:o_kr], kvn_ref[...]).astype(BF16)
    qb = _dot(cq, wq_ref[...])
    kv = _dot(ckv, wkv_ref[...])
    k_pe = _rotate(proj[:, o_kr:o_kr + LANES], cb, sb)
    q_scale = (B_NOPE + B_ROPE) ** -0.5 * LOG2_E
    for hd in range(B_HEADS):
        qb_ref[hd] = (_rotate(qb[:, blk(0, hd)], cb, sb) * q_scale).astype(BF16)
        kb_ref[hd] = (kv[:, blk(0, hd)] + k_pe).astype(BF16)
        vb_ref[hd] = kv[:, blk(B_HEADS * LANES, hd)].astype(BF16)


def _even_proj(h, g, w_in, pos, inv, q_norm, w_q, kv_norm, w_kv, *, tm):
    t, d = h.shape
    row = lambda w: pl.BlockSpec((tm, w), lambda i: (i, 0))
    heads = lambda n: pl.BlockSpec((n, tm, LANES), lambda i: (0, i, 0))
    shp = lambda n: jax.ShapeDtypeStruct((n, t, LANES), BF16)
    return pl.pallas_call(
        _even_proj_body,
        grid=(t // tm,),
        in_specs=[row(d), _resident((1, d)), _resident(w_in.shape), row(1), _resident(inv.shape),
                  _resident(q_norm.shape), _resident(w_q.shape), _resident(kv_norm.shape), _resident(w_kv.shape)],
        out_specs=[heads(A_HEADS)] * 3 + [heads(B_HEADS)] * 3,
        out_shape=[shp(A_HEADS)] * 3 + [shp(B_HEADS)] * 3,
        compiler_params=_params("parallel"),
        name="even_proj",
    )(h, g, w_in, pos, inv, q_norm, w_q, kv_norm, w_kv)


A_TILE = A_BACK * max(A_DILATIONS)
A_STEP = 4
assert A_DILATIONS == (1, A_STEP, A_STEP * A_STEP)
A_PHASE4 = A_TILE // A_STEP


def _dilated_body(q_ref, k_ref, v_ref, o_ref, nat, p4f, q4, q16, k1, k4, k16, v1, v4, v16, o_s, lse_s, bias):
    i = pl.program_id(2)

    @pl.when(i > 0)
    def _():
        for buf in (k1, v1):
            buf[0:A_BACK] = buf[A_TILE:A_TILE + A_BACK]
        for buf in (k4, v4):
            buf[:, 0:A_BACK] = buf[:, A_PHASE4:A_PHASE4 + A_BACK]
        for buf in (k16, v16):
            buf[:, 0:A_BACK] = buf[:, A_BACK:2 * A_BACK]

    @pl.when(i == 0)
    def _():
        for buf in (k1, v1):
            buf[0:A_BACK] = jnp.zeros((A_BACK, HEAD_DIM), BF16)
        for buf in (k4, v4, k16, v16):
            buf[:, 0:A_BACK] = jnp.zeros((buf.shape[0], A_BACK, HEAD_DIM), BF16)

    def regroup(x_ref, d4, d16, halo):
        nat[...] = x_ref[0].astype(F32)
        for r4 in range(A_STEP):
            rows = nat[pl.ds(r4, A_PHASE4, stride=A_STEP), :]
            p4f[r4 * A_PHASE4:(r4 + 1) * A_PHASE4] = rows
            d4[r4, halo:halo + A_PHASE4] = rows.astype(BF16)
        for r16 in range(A_STEP * A_STEP):
            r4, sub = r16 % A_STEP, r16 // A_STEP
            rows = p4f[pl.ds(r4 * A_PHASE4 + sub, A_BACK, stride=A_STEP), :]
            d16[r16, halo:halo + A_BACK] = rows.astype(BF16)

    regroup(q_ref, q4, q16, 0)
    k1[A_BACK:A_BACK + A_TILE] = k_ref[0]
    regroup(k_ref, k4, k16, A_BACK)
    v1[A_BACK:A_BACK + A_TILE] = v_ref[0]
    regroup(v_ref, v4, v16, A_BACK)

    qi = lax.broadcasted_iota(jnp.int32, (A_BACK, 2 * A_BACK), 0)
    ki = lax.broadcasted_iota(jnp.int32, (A_BACK, 2 * A_BACK), 1)
    band = (ki >= qi) & (ki <= qi + A_BACK)
    bias[0] = jnp.where(band, 0.0, MASK_VALUE)
    bias[1] = jnp.where(band & (ki >= jnp.where(i > 0, 0, A_BACK)), 0.0, MASK_VALUE)

    def band_block(p, qb, kb, vb, first, out_rows):
        s = _dot_nt(qb, kb) + bias[1 if first else 0]
        m = jnp.max(s, axis=-1, keepdims=True)
        e = jnp.exp2(s - m).astype(BF16)
        res = _dot(e, jnp.concatenate([vb, jnp.ones_like(vb)], axis=1))
        den = res[:, HEAD_DIM:]
        o_s[p, out_rows, :] = res[:, :HEAD_DIM] / den
        lse_s[p, out_rows, :] = m + jnp.log2(den)

    @pl.when(i >= 0)
    def _():
        for j in range(A_TILE // A_BACK):
            blk, bnd = slice(j * A_BACK, (j + 1) * A_BACK), slice(j * A_BACK, (j + 2) * A_BACK)
            band_block(0, q_ref[0, blk, :], k1[bnd], v1[bnd], j == 0, blk)
        for r4 in range(A_STEP):
            for j in range(A_PHASE4 // A_BACK):
                blk, bnd = slice(j * A_BACK, (j + 1) * A_BACK), slice(j * A_BACK, (j + 2) * A_BACK)
                band_block(1, q4[r4, blk], k4[r4, bnd], v4[r4, bnd], j == 0,
                           pl.ds(r4 + j * A_BACK * A_STEP, A_BACK, stride=A_STEP))
        for r16 in range(A_STEP * A_STEP):
            band_block(2, q16[r16], k16[r16], v16[r16], True, pl.ds(r16, A_BACK, stride=A_STEP * A_STEP))

    n_pat = len(A_DILATIONS)
    lse = [lse_s[p] for p in range(n_pat)]
    top = functools.reduce(jnp.maximum, lse)
    wts = [jnp.exp2(x - top) for x in lse]
    total = functools.reduce(jnp.add, wts)
    out = functools.reduce(jnp.add, [w * o_s[p] for p, w in enumerate(wts)]) / total
    o_ref[...] = out.astype(BF16)


def _dilated_attn(q, k, v, *, batch, seq):
    n_tiles = seq // A_TILE
    heads, t, _ = q.shape
    blk = pl.BlockSpec((1, A_TILE, HEAD_DIM), lambda b, h, i: (h, b * n_tiles + i, 0))
    n16 = A_STEP * A_STEP
    return pl.pallas_call(
        _dilated_body,
        grid=(batch, heads, n_tiles),
        in_specs=[blk, blk, blk],
        out_specs=pl.BlockSpec((A_TILE, HEAD_DIM), lambda b, h, i: (b * n_tiles + i, h)),
        out_shape=jax.ShapeDtypeStruct((t, heads * HEAD_DIM), BF16),
        scratch_shapes=[
            pltpu.VMEM((A_TILE, HEAD_DIM), F32),
            pltpu.VMEM((A_TILE, HEAD_DIM), F32),
            pltpu.VMEM((A_STEP, A_PHASE4, HEAD_DIM), BF16),
            pltpu.VMEM((n16, A_BACK, HEAD_DIM), BF16),
        ] + [
            pltpu.VMEM((A_BACK + A_TILE, HEAD_DIM), BF16),
            pltpu.VMEM((A_STEP, A_BACK + A_PHASE4, HEAD_DIM), BF16),
            pltpu.VMEM((n16, 2 * A_BACK, HEAD_DIM), BF16),
        ] * 2 + [
            pltpu.VMEM((len(A_DILATIONS), A_TILE, HEAD_DIM), F32),
            pltpu.VMEM((len(A_DILATIONS), A_TILE, HEAD_DIM), F32),
            pltpu.VMEM((2, A_BACK, 2 * A_BACK), F32),
        ],
        compiler_params=_params("parallel", "parallel", "arbitrary"),
        name="dilated_attn",
    )(q, k, v)


def _mla_body(q_ref, k_ref, v_ref, o_ref, s_a, s_b, mx_a, mx_b, al_s, m_s, l_s, acc_s, *, tq, tk):
    qi = pl.program_id(2)
    n_tiles = tk // LANES
    n_blocks = 2 * qi + 2
    m_s[...] = jnp.full(m_s.shape, MASK_VALUE, F32)
    l_s[...] = jnp.zeros(l_s.shape, F32)
    acc_s[...] = jnp.zeros(acc_s.shape, F32)

    def key_rows(n):
        return pl.ds(pl.multiple_of(n * tk, tk), tk)

    def logits(n, s_ref, mx_ref, diag_block, r0=0):
        s = _dot_nt(q_ref[0, r0:, :], k_ref[0, key_rows(n), :])
        tiles = [s[:, c * LANES:(c + 1) * LANES] for c in range(n_tiles)]
        if diag_block is not None:
            row = lax.broadcasted_iota(jnp.int32, (tq - r0, LANES), 0) + r0
            col = lax.broadcasted_iota(jnp.int32, (tq - r0, LANES), 1) + diag_block * tk
            tiles = [jnp.where(col + c * LANES <= row, x, MASK_VALUE) for c, x in enumerate(tiles)]
        for c, x in enumerate(tiles):
            s_ref[r0:, c * LANES:(c + 1) * LANES] = x
        mx = jnp.max(functools.reduce(jnp.maximum, tiles), axis=-1, keepdims=True)
        mx_ref[r0:, :] = jnp.broadcast_to(mx, (tq - r0, LANES))

    def softmax_av(s_ref, mx_ref, n, r0=0):
        m_prev = m_s[r0:, :]
        m_new = jnp.maximum(m_prev, mx_ref[r0:, :])
        m_s[r0:, :] = m_new
        al_s[r0:, :] = jnp.exp2(m_prev - m_new)
        pb = jnp.concatenate([jnp.exp2(s_ref[r0:, c * LANES:(c + 1) * LANES] - m_s[r0:, :]).astype(BF16)
                              for c in range(n_tiles)], axis=1)
        v = v_ref[0, key_rows(n), :]
        res = _dot(pb, jnp.concatenate([v, jnp.ones_like(v)], axis=1))
        l_s[r0:, :] = al_s[r0:, :] * l_s[r0:, :] + res[:, LANES:]
        acc_s[r0:, :] = al_s[r0:, :] * acc_s[r0:, :] + res[:, :LANES]

    @pl.when(qi > 0)
    def _():
        logits(0, s_a, mx_a, None)

    def pair(u, carry):
        logits(2 * u + 1, s_b, mx_b, None)
        softmax_av(s_a, mx_a, 2 * u)
        logits(2 * u + 2, s_a, mx_a, None)
        softmax_av(s_b, mx_b, 2 * u + 1)
        return carry

    lax.fori_loop(0, qi - 1, pair, 0)

    @pl.when(qi > 0)
    def _():
        logits(n_blocks - 3, s_b, mx_b, None)
        softmax_av(s_a, mx_a, n_blocks - 4)
        logits(n_blocks - 2, s_a, mx_a, 0)
        softmax_av(s_b, mx_b, n_blocks - 3)

    @pl.when(qi == 0)
    def _():
        logits(0, s_a, mx_a, 0)

    logits(n_blocks - 1, s_b, mx_b, 1, r0=tk)
    softmax_av(s_a, mx_a, n_blocks - 2)
    softmax_av(s_b, mx_b, n_blocks - 1, r0=tk)
    o_ref[...] = (acc_s[...] / l_s[...]).astype(BF16)


def _mla_attn(q, k, v, *, batch, seq, tq, tk):
    assert tq == 2 * tk
    heads, t, _ = q.shape
    nq = seq // tq
    kv = pl.BlockSpec((1, seq, LANES), lambda h, b, i: (h, b, 0))
    return pl.pallas_call(
        functools.partial(_mla_body, tq=tq, tk=tk),
        grid=(heads, batch, nq),
        in_specs=[pl.BlockSpec((1, tq, LANES), lambda h, b, i: (h, b * nq + i, 0)), kv, kv],
        out_specs=pl.BlockSpec((tq, LANES), lambda h, b, i: (b * nq + i, h)),
        out_shape=jax.ShapeDtypeStruct((t, heads * LANES), BF16),
        scratch_shapes=[pltpu.VMEM((tq, tk), F32)] * 2 + [pltpu.VMEM((tq, LANES), F32)] * 6,
        compiler_params=_params("parallel", "parallel", "arbitrary"),
        name="mla_attn",
    )(q, k, v)


def _odd_proj_body(h_ref, g_ref, win_ref, pw_ref, ps_ref, oc_ref, qd_ref, kd_ref, vd_ref, ext, *, tm, tiles_per_seq):
    i = pl.program_id(0)
    proj, = _rms_dot(h_ref[...], g_ref[...], win_ref[...])

    @pl.when(i % tiles_per_seq == 0)
    def _():
        ext[0:C_HALO, :] = jnp.zeros((C_HALO, C_WIDTH), F32)

    ext[C_HALO:C_HALO + tm, :] = proj[:, 0:C_WIDTH]
    t_in_seq = (i % tiles_per_seq) * tm + lax.broadcasted_iota(jnp.int32, (tm, 1), 0)
    for g, w in enumerate(C_WINDOWS):
        cols = slice(g * C_GROUP_DIM, (g + 1) * C_GROUP_DIM)
        s = ext[C_HALO - (w - 1):C_HALO + tm, cols]
        step = 1
        while step < w:
            s = s[step:] + s[:-step]
            step *= 2
        count = jnp.minimum(t_in_seq + 1, w).astype(F32)
        pooled = s / count - proj[:, cols]
        oc_ref[:, cols] = (_dot(pooled.astype(BF16), pw_ref[g]) * ps_ref[:, cols]).astype(BF16)
    ext[0:C_HALO, :] = ext[tm:tm + C_HALO, :]

    dw = D_HEADS * HEAD_DIM
    for hd in range(D_HEADS):
        lanes = lambda base: slice(base + hd * LANES, base + (hd + 1) * LANES)
        qd_ref[hd] = (proj[:, lanes(C_WIDTH)] * (HEAD_DIM ** -0.5 * LOG2_E)).astype(BF16)
        kd_ref[hd] = proj[:, lanes(C_WIDTH + dw)].astype(BF16)
        vd_ref[hd] = proj[:, lanes(C_WIDTH + 2 * dw)].astype(BF16)


def _odd_proj(h, g, w_in, pool_w, pool_scale, *, seq, tm):
    t, d = h.shape
    row = lambda w: pl.BlockSpec((tm, w), lambda i: (i, 0))
    heads = pl.BlockSpec((D_HEADS, tm, LANES), lambda i: (0, i, 0))
    shp = jax.ShapeDtypeStruct((D_HEADS, t, LANES), BF16)
    return pl.pallas_call(
        functools.partial(_odd_proj_body, tm=tm, tiles_per_seq=seq // tm),
        grid=(t // tm,),
        in_specs=[row(d), _resident((1, d)), _resident(w_in.shape), _resident(pool_w.shape), _resident(pool_scale.shape)],
        out_specs=[row(C_WIDTH), heads, heads, heads],
        out_shape=[jax.ShapeDtypeStruct((t, C_WIDTH), BF16), shp, shp, shp],
        scratch_shapes=[pltpu.VMEM((C_HALO + tm, C_WIDTH), F32)],
        compiler_params=_params("arbitrary"),
        name="odd_proj",
    )(h, g, w_in, pool_w, pool_scale)


SB_SUB = 256
SB_DEAD_CARRY = 160.0


def _sb_body(q_ref, k_ref, v_ref, o_ref, z_a, z_b, e_a, e_b, carry_s, acc_s, *, tq, tk, heads):
    qi = pl.program_id(2)
    n_blocks = 2 * qi + 2
    carry_s[...] = jnp.zeros(carry_s.shape, F32)
    acc_s[...] = jnp.zeros(acc_s.shape, F32)
    later = (lax.broadcasted_iota(jnp.int32, (SB_SUB, SB_SUB), 0) > lax.broadcasted_iota(jnp.int32, (SB_SUB, SB_SUB), 1))
    later = jnp.where(later, 1.0, 0.0).astype(BF16)

    def key_rows(n):
        kb = jnp.maximum(n_blocks - 1 - n, 0)
        return pl.ds(pl.multiple_of(kb * tk, tk), tk)

    def logits(n, z_ref, r0=0):
        for h in range(heads):
            z_ref[h, r0:, :] = _dot_nt(q_ref[h, r0:, :], k_ref[h, key_rows(n), :])

    def gate(z_ref, e_ref, diag_block, r0=0):
        for h in range(heads):
            carry = carry_s[h, r0:, :]
            for sub in reversed(range(tk // SB_SUB)):
                z = z_ref[h, r0:, sub * SB_SUB:(sub + 1) * SB_SUB]
                softplus = jnp.maximum(z, 0.0) + jnp.log2(1.0 + jnp.exp2(-jnp.abs(z)))
                arg = z - softplus
                if diag_block is None:
                    keep = softplus
                else:
                    row = lax.broadcasted_iota(jnp.int32, (tq - r0, SB_SUB), 0) + r0
                    col = lax.broadcasted_iota(jnp.int32, (tq - r0, SB_SUB), 1) + diag_block * tk + sub * SB_SUB
                    keep = jnp.where(col < row, softplus, 0.0)
                    arg = jnp.where(col < row, arg, MASK_VALUE)
                expo = arg - _dot(keep.astype(BF16), later)
                for c in range(SB_SUB // LANES):
                    lo = sub * SB_SUB + c * LANES
                    e_ref[h, r0:, lo:lo + LANES] = expo[:, c * LANES:(c + 1) * LANES] - carry
                carry = carry + jnp.sum(keep, axis=-1, keepdims=True)
            carry_s[h, r0:, :] = carry

    def weigh(e_ref, n, r0=0):
        for h in range(heads):
            acc_s[h, r0:, :] += _dot(jnp.exp2(e_ref[h, r0:, :]).astype(BF16), v_ref[h, key_rows(n), :])

    logits(0, z_a, r0=tk)
    logits(1, z_b)
    gate(z_a, e_a, 1, r0=tk)
    logits(2, z_a)
    gate(z_b, e_b, 0)
    weigh(e_a, 0, r0=tk)

    def step(n, z_this, e_this, z_next, e_prev):
        logits(n + 1, z_next)
        gate(z_this, e_this, None)
        weigh(e_prev, n - 1)

    def one_block(state):
        n, _ = state

        @pl.when(n % 2 == 0)
        def _():
            step(n, z_a, e_a, z_b, e_b)

        @pl.when(n % 2 == 1)
        def _():
            step(n, z_b, e_b, z_a, e_a)

        return n + 1, jnp.min(carry_s[...])

    def more_blocks(state):
        n, min_carry = state
        return (n < n_blocks) & (min_carry < SB_DEAD_CARRY)

    n_done, _ = lax.while_loop(more_blocks, one_block, (jnp.int32(2), jnp.min(carry_s[...])))

    @pl.when(n_done % 2 == 0)
    def _():
        weigh(e_b, n_done - 1)

    @pl.when(n_done % 2 == 1)
    def _():
        weigh(e_a, n_done - 1)

    for h in range(heads):
        o_ref[:, h * LANES:(h + 1) * LANES] = acc_s[h].astype(BF16)


def _sb_attn(q, k, v, *, batch, seq, tq, tk, heads_per_step):
    assert tq == 2 * tk and tk % SB_SUB == 0
    heads, t, _ = q.shape
    hp = heads_per_step
    nq = seq // tq
    kv = pl.BlockSpec((hp, seq, LANES), lambda h, b, i: (h, b, 0), pipeline_mode=pl.Buffered(1))
    return pl.pallas_call(
        functools.partial(_sb_body, tq=tq, tk=tk, heads=hp),
        grid=(heads // hp, batch, nq),
        in_specs=[pl.BlockSpec((hp, tq, LANES), lambda h, b, i: (h, b * nq + i, 0)), kv, kv],
        out_specs=pl.BlockSpec((tq, hp * LANES), lambda h, b, i: (b * nq + i, h)),
        out_shape=jax.ShapeDtypeStruct((t, heads * LANES), BF16),
        scratch_shapes=[pltpu.VMEM((hp, tq, tk), F32)] * 4 + [pltpu.VMEM((hp, tq, LANES), F32)] * 2,
        compiler_params=_params("parallel", "parallel", "arbitrary"),
        name="sb_attn",
    )(q, k, v)


def _latent_head_lanes():
    half_pe, half_nope = B_ROPE // 2, B_NOPE // 2
    pe = np.concatenate([np.arange(half_pe), LANES // 2 + np.arange(half_pe)])
    nope = np.concatenate([half_pe + np.arange(half_nope), LANES // 2 + half_pe + np.arange(half_nope)])
    return pe, nope


def _scatter_cols(w, src_cols, dst_cols, width):
    out = jnp.zeros((w.shape[0], width), w.dtype)
    return out.at[:, np.asarray(dst_cols)].set(w[:, np.asarray(src_cols)])


def _even_weights(w_in, w_q_up, w_kv_up):
    pe, nope = _latent_head_lanes()
    a_w = A_HEADS * HEAD_DIM
    o_kr = 3 * a_w + B_Q_LORA + B_KV_LORA
    k_rope = _scatter_cols(w_in, o_kr + np.arange(B_ROPE), pe, LANES)
    w_in_p = jnp.concatenate([w_in[:, :o_kr], k_rope], axis=1).astype(BF16)
    qd, kvd = B_NOPE + B_ROPE, B_NOPE + B_V
    src_q, dst_q, src_k, dst_k, src_v = [], [], [], [], []
    for hd in range(B_HEADS):
        src_q += list(hd * qd + np.arange(B_NOPE)) + list(hd * qd + B_NOPE + np.arange(B_ROPE))
        dst_q += list(hd * LANES + nope) + list(hd * LANES + pe)
        src_k += list(hd * kvd + np.arange(B_NOPE))
        dst_k += list(hd * LANES + nope)
        src_v += list(hd * kvd + B_NOPE + np.arange(B_V))
    w_q = _scatter_cols(w_q_up, src_q, dst_q, B_HEADS * LANES).astype(BF16)
    w_k = _scatter_cols(w_kv_up, src_k, dst_k, B_HEADS * LANES)
    w_kv = jnp.concatenate([w_k, w_kv_up[:, np.asarray(src_v)]], axis=1).astype(BF16)
    return w_in_p, w_q, w_kv


def _rope_frequencies():
    inv_a = ROPE_THETA ** (-jnp.arange(0, HEAD_DIM, 2, dtype=F32) / HEAD_DIM)
    inv_b = ROPE_THETA ** (-jnp.arange(0, B_ROPE, 2, dtype=F32) / B_ROPE)
    pad = jnp.zeros((LANES - inv_a.shape[0] - inv_b.shape[0],), F32)
    return jnp.concatenate([inv_a, inv_b, pad])[None, :]


def kernel(x, positions, norm_g, ffn_w_gate, ffn_w_up, ffn_w_down, even_w_in, even_q_norm, even_w_q_up, even_kv_norm, even_w_kv_up, even_w_out, odd_w_in, odd_pool_w, odd_pool_scale, odd_w_out, final_norm):
    batch, seq, d = x.shape
    t = batch * seq
    assert seq % A_TILE == 0
    tm = 512
    h = x.reshape(t, d).astype(F32)
    pos = positions.reshape(t, 1).astype(jnp.int32)
    bf = lambda w: w.astype(BF16)
    gain = lambda i, j: norm_g[i, j][None, :].astype(F32)
    wg, wu, wd = bf(ffn_w_gate), bf(ffn_w_up), bf(ffn_w_down)
    ffn = functools.partial(_ffn, wg=wg, wu=wu, wd=wd, tm=tm)

    h = ffn(h, [], None, gain(0, 0), which=(0, 0))
    w_in_p, w_q, w_kv = _even_weights(even_w_in[0], even_w_q_up[0], even_w_kv_up[0])
    qa, ka, va, qb, kb, vb = _even_proj(h, gain(0, 1), w_in_p, pos, _rope_frequencies(), even_q_norm[0][None, :].astype(F32), w_q,
                                        even_kv_norm[0][None, :].astype(F32), w_kv, tm=tm)
    out_a = _dilated_attn(qa, ka, va, batch=batch, seq=seq)
    out_b = _mla_attn(qb, kb, vb, batch=batch, seq=seq, tq=1024, tk=512)
    h = ffn(h, [out_a, out_b], bf(even_w_out[0]), gain(0, 2), which=(0, 1))

    h = ffn(h, [], None, gain(1, 0), which=(1, 0))
    out_c, qd, kd, vd = _odd_proj(h, gain(1, 1), bf(odd_w_in[0]), bf(odd_pool_w[0]),
                                  odd_pool_scale[0][None, :].astype(F32), seq=seq, tm=tm)
    out_d = _sb_attn(qd, kd, vd, batch=batch, seq=seq, tq=512, tk=256, heads_per_step=4)
    h = ffn(h, [out_c, out_d], bf(odd_w_out[0]), gain(1, 2), which=(1, 1), final_g=final_norm[None, :].astype(F32))
    return h.reshape(batch, seq, d).astype(x.dtype)
```

```python
import functools

import jax
import jax.numpy as jnp
import numpy as np
from jax import lax
from jax.experimental import pallas as pl
from jax.experimental.pallas import tpu as pltpu

F32 = jnp.float32
BF16 = jnp.bfloat16

HEAD_DIM = 128
LANES = 128
ROPE_THETA = 10000.0
NORM_EPS = 1e-6
A_HEADS = 6
A_DILATIONS = (1, 4, 16)
A_BACK = 128
B_HEADS = 4
B_Q_LORA = 384
B_KV_LORA = 256
B_NOPE = 64
B_ROPE = 32
B_V = 128
C_WINDOWS = (2, 4, 8, 16)
C_GROUP_DIM = 128
C_WIDTH = len(C_WINDOWS) * C_GROUP_DIM
C_HALO = 16
D_HEADS = 4
MASK_VALUE = -1e30
LOG2_E = 1.4426950408889634
VMEM_LIMIT_BYTES = 56 * 1024 * 1024


def _params(*semantics):
    return pltpu.CompilerParams(dimension_semantics=semantics, vmem_limit_bytes=VMEM_LIMIT_BYTES)


def _resident(shape):
    return pl.BlockSpec(shape, lambda *_: (0,) * len(shape), pipeline_mode=pl.Buffered(1))


def _rms(x, g):
    ms = jnp.mean(x * x, axis=-1, keepdims=True)
    return x * lax.rsqrt(ms + NORM_EPS) * g


def _dot(a, b):
    return jnp.dot(a, b, preferred_element_type=F32)


def _rms_dot(x, g, *weights):
    r = lax.rsqrt(jnp.mean(x * x, axis=-1, keepdims=True) + NORM_EPS)
    xg = (x * g).astype(BF16)
    return [r * _dot(xg, w) for w in weights]


def _dot_nt(a, b):
    return lax.dot_general(a, b, (((1,), (1,)), ((), ())), preferred_element_type=F32)


FFN_CAST_ROWS = 128


def _load_bf16(src, dst, stage, sem):
    rows, cols = src.shape
    n = rows // FFN_CAST_ROWS

    def chunk(c):
        return pltpu.make_async_copy(src.at[pl.ds(c * FFN_CAST_ROWS, FFN_CAST_ROWS), :],
                                     stage.at[c % 2, :, pl.ds(0, cols)], sem.at[c % 2])

    chunk(0).start()
    for c in range(n):
        if c + 1 < n:
            chunk(c + 1).start()
        chunk(c).wait()
        dst[c * FFN_CAST_ROWS:(c + 1) * FFN_CAST_ROWS, :] = stage[c % 2, :, 0:cols].astype(BF16)


def _ffn_body(*refs, mix_widths, final_norm, which):
    h_ref = refs[0]
    n_mix = len(mix_widths)
    mix_refs = refs[1:1 + n_mix]
    base = 1 + n_mix
    wo_ref = refs[base] if n_mix else None
    base += 1 if n_mix else 0
    g_ref, wg_hbm, wu_hbm, wd_hbm = refs[base:base + 4]
    fg_ref = refs[base + 4] if final_norm else None
    o_ref, wg_ref, wu_ref, wd_ref, stage, sem = refs[-6:]

    @pl.when(pl.program_id(0) == 0)
    def _():
        _load_bf16(wg_hbm.at[which[0], which[1]], wg_ref, stage, sem)
        _load_bf16(wu_hbm.at[which[0], which[1]], wu_ref, stage, sem)
        _load_bf16(wd_hbm.at[which[0], which[1]], wd_ref, stage, sem)

    h = h_ref[...]
    row = 0
    for m_ref, width in zip(mix_refs, mix_widths):
        h = h + _dot(m_ref[...], wo_ref[row:row + width, :])
        row += width
    gate, up = _rms_dot(h, g_ref[...], wg_ref[...], wu_ref[...])
    act = (gate * jax.nn.sigmoid(gate) * up).astype(BF16)
    y = h + 0.5 * _dot(act, wd_ref[...])
    if final_norm:
        y = _rms(y, fg_ref[...])
    o_ref[...] = y


def _ffn(h, mix, w_out, g, wg, wu, wd, which, final_g=None, *, tm):
    t, d = h.shape
    d_ff = wg.shape[-1]
    row = lambda w: pl.BlockSpec((tm, w), lambda i: (i, 0))
    hbm = pl.BlockSpec(memory_space=pl.ANY)
    args, specs = [h], [row(d)]
    for m in mix:
        args.append(m)
        specs.append(row(m.shape[1]))
    if mix:
        args.append(w_out)
        specs.append(_resident(w_out.shape))
    args += [g, wg, wu, wd]
    specs += [_resident((1, d)), hbm, hbm, hbm]
    if final_g is not None:
        args.append(final_g)
        specs.append(_resident((1, d)))
    return pl.pallas_call(
        functools.partial(_ffn_body, mix_widths=tuple(m.shape[1] for m in mix), final_norm=final_g is not None,
                          which=which),
        grid=(t // tm,),
        in_specs=specs,
        out_specs=row(d),
        out_shape=jax.ShapeDtypeStruct((t, d), F32),
        scratch_shapes=[pltpu.VMEM((d, d_ff), BF16), pltpu.VMEM((d, d_ff), BF16), pltpu.VMEM((d_ff, d), BF16),
                        pltpu.VMEM((2, FFN_CAST_ROWS, max(d, d_ff)), F32), pltpu.SemaphoreType.DMA((2,))],
        compiler_params=_params("arbitrary"),
        name="ffn",
    )(*args)


def _rope_tables(pos, inv):
    lane = lax.broadcasted_iota(jnp.int32, (1, LANES), 1)
    half = LANES // 2
    sign = jnp.where(lane < half, -1.0, 1.0).astype(F32)
    ang = pos * inv
    c, s = jnp.cos(ang), jnp.sin(ang)
    c_sw, s_sw = pltpu.roll(c, half, 1), pltpu.roll(s, half, 1)
    first, second = lane < B_ROPE // 2, (lane >= half) & (lane < half + B_ROPE // 2)
    return (jnp.where(lane < half, c, c_sw), sign * jnp.where(lane < half, s, s_sw),
            jnp.where(first, c_sw, jnp.where(second, c, 1.0)), sign * jnp.where(first, s_sw, jnp.where(second, s, 0.0)))


def _rotate(x, c, s):
    return x * c + pltpu.roll(x, LANES // 2, 1) * s


def _even_proj_body(h_ref, g_ref, win_ref, pos_ref, inv_ref, qn_ref, wq_ref, kvn_ref, wkv_ref,
                    qa_ref, ka_ref, va_ref, qb_ref, kb_ref, vb_ref):
    xn = _rms(h_ref[...], g_ref[...]).astype(BF16)
    proj = _dot(xn, win_ref[...])
    ca, sa, cb, sb = _rope_tables(pos_ref[...].astype(F32), inv_ref[...])
    a_w = A_HEADS * HEAD_DIM
    blk = lambda base, i: slice(base + i * LANES, base + (i + 1) * LANES)
    for hd in range(A_HEADS):
        qa_ref[hd] = (_rotate(proj[:, blk(0, hd)], ca, sa) * (HEAD_DIM ** -0.5 * LOG2_E)).astype(BF16)
        ka_ref[hd] = _rotate(proj[:, blk(a_w, hd)], ca, sa).astype(BF16)
        va_ref[hd] = proj[:, blk(2 * a_w, hd)].astype(BF16)
    o_cq = 3 * a_w
    o_ckv = o_cq + B_Q_LORA
    o_kr = o_ckv + B_KV_LORA
    cq = _rms(proj[:, o_cq:o_ckv], qn_ref[...]).astype(BF16)
    ckv = _rms(proj[:, o_ckv:o_kr], kvn_ref[...]).astype(BF16)
    qb = _dot(cq, wq_ref[...])
    kv = _dot(ckv, wkv_ref[...])
    k_pe = _rotate(proj[:, o_kr:o_kr + LANES], cb, sb)
    q_scale = (B_NOPE + B_ROPE) ** -0.5 * LOG2_E
    for hd in range(B_HEADS):
        qb_ref[hd] = (_rotate(qb[:, blk(0, hd)], cb, sb) * q_scale).astype(BF16)
        kb_ref[hd] = (kv[:, blk(0, hd)] + k_pe).astype(BF16)
        vb_ref[hd] = kv[:, blk(B_HEADS * LANES, hd)].astype(BF16)


def _even_proj(h, g, w_in, pos, inv, q_norm, w_q, kv_norm, w_kv, *, tm):
    t, d = h.shape
    row = lambda w: pl.BlockSpec((tm, w), lambda i: (i, 0))
    heads = lambda n: pl.BlockSpec((n, tm, LANES), lambda i: (0, i, 0))
    shp = lambda n: jax.ShapeDtypeStruct((n, t, LANES), BF16)
    return pl.pallas_call(
        _even_proj_body,
        grid=(t // tm,),
        in_specs=[row(d), _resident((1, d)), _resident(w_in.shape), row(1), _resident(inv.shape),
                  _resident(q_norm.shape), _resident(w_q.shape), _resident(kv_norm.shape), _resident(w_kv.shape)],
        out_specs=[heads(A_HEADS)] * 3 + [heads(B_HEADS)] * 3,
        out_shape=[shp(A_HEADS)] * 3 + [shp(B_HEADS)] * 3,
        compiler_params=_params("parallel"),
        name="even_proj",
    )(h, g, w_in, pos, inv, q_norm, w_q, kv_norm, w_kv)


A_TILE = A_BACK * max(A_DILATIONS)
A_STEP = 4
assert A_DILATIONS == (1, A_STEP, A_STEP * A_STEP)
A_PHASE4 = A_TILE // A_STEP


def _dilated_body(q_ref, k_ref, v_ref, o_ref, nat, p4f, q4, q16, k1, k4, k16, v1, v4, v16, o_s, lse_s, bias):
    i = pl.program_id(2)

    @pl.when(i > 0)
    def _():
        for buf in (k1, v1):
            buf[0:A_BACK] = buf[A_TILE:A_TILE + A_BACK]
        for buf in (k4, v4):
            buf[:, 0:A_BACK] = buf[:, A_PHASE4:A_PHASE4 + A_BACK]
        for buf in (k16, v16):
            buf[:, 0:A_BACK] = buf[:, A_BACK:2 * A_BACK]

    @pl.when(i == 0)
    def _():
        for buf in (k1, v1):
            buf[0:A_BACK] = jnp.zeros((A_BACK, HEAD_DIM), BF16)
        for buf in (k4, v4, k16, v16):
            buf[:, 0:A_BACK] = jnp.zeros((buf.shape[0], A_BACK, HEAD_DIM), BF16)

    def regroup(x_ref, d4, d16, halo):
        nat[...] = x_ref[0].astype(F32)
        for r4 in range(A_STEP):
            rows = nat[pl.ds(r4, A_PHASE4, stride=A_STEP), :]
            p4f[r4 * A_PHASE4:(r4 + 1) * A_PHASE4] = rows
            d4[r4, halo:halo + A_PHASE4] = rows.astype(BF16)
        for r16 in range(A_STEP * A_STEP):
            r4, sub = r16 % A_STEP, r16 // A_STEP
            rows = p4f[pl.ds(r4 * A_PHASE4 + sub, A_BACK, stride=A_STEP), :]
            d16[r16, halo:halo + A_BACK] = rows.astype(BF16)

    regroup(q_ref, q4, q16, 0)
    k1[A_BACK:A_BACK + A_TILE] = k_ref[0]
    regroup(k_ref, k4, k16, A_BACK)
    v1[A_BACK:A_BACK + A_TILE] = v_ref[0]
    regroup(v_ref, v4, v16, A_BACK)

    qi = lax.broadcasted_iota(jnp.int32, (A_BACK, 2 * A_BACK), 0)
    ki = lax.broadcasted_iota(jnp.int32, (A_BACK, 2 * A_BACK), 1)
    band = (ki >= qi) & (ki <= qi + A_BACK)
    bias[0] = jnp.where(band, 0.0, MASK_VALUE)
    bias[1] = jnp.where(band & (ki >= jnp.where(i > 0, 0, A_BACK)), 0.0, MASK_VALUE)

    def band_block(p, qb, kb, vb, first, out_rows):
        s = _dot_nt(qb, kb) + bias[1 if first else 0]
        m = jnp.max(s, axis=-1, keepdims=True)
        e = jnp.exp2(s - m).astype(BF16)
        res = _dot(e, jnp.concatenate([vb, jnp.ones_like(vb)], axis=1))
        den = res[:, HEAD_DIM:]
        o_s[p, out_rows, :] = res[:, :HEAD_DIM] / den
        lse_s[p, out_rows, :] = m + jnp.log2(den)

    @pl.when(i >= 0)
    def _():
        for j in range(A_TILE // A_BACK):
            blk, bnd = slice(j * A_BACK, (j + 1) * A_BACK), slice(j * A_BACK, (j + 2) * A_BACK)
            band_block(0, q_ref[0, blk, :], k1[bnd], v1[bnd], j == 0, blk)
        for r4 in range(A_STEP):
            for j in range(A_PHASE4 // A_BACK):
                blk, bnd = slice(j * A_BACK, (j + 1) * A_BACK), slice(j * A_BACK, (j + 2) * A_BACK)
                band_block(1, q4[r4, blk], k4[r4, bnd], v4[r4, bnd], j == 0,
                           pl.ds(r4 + j * A_BACK * A_STEP, A_BACK, stride=A_STEP))
        for r16 in range(A_STEP * A_STEP):
            band_block(2, q16[r16], k16[r16], v16[r16], True, pl.ds(r16, A_BACK, stride=A_STEP * A_STEP))

    n_pat = len(A_DILATIONS)
    lse = [lse_s[p] for p in range(n_pat)]
    top = functools.reduce(jnp.maximum, lse)
    wts = [jnp.exp2(x - top) for x in lse]
    total = functools.reduce(jnp.add, wts)
    out = functools.reduce(jnp.add, [w * o_s[p] for p, w in enumerate(wts)]) / total
    o_ref[...] = out.astype(BF16)


def _dilated_attn(q, k, v, *, batch, seq):
    n_tiles = seq // A_TILE
    heads, t, _ = q.shape
    blk = pl.BlockSpec((1, A_TILE, HEAD_DIM), lambda b, h, i: (h, b * n_tiles + i, 0))
    n16 = A_STEP * A_STEP
    return pl.pallas_call(
        _dilated_body,
        grid=(batch, heads, n_tiles),
        in_specs=[blk, blk, blk],
        out_specs=pl.BlockSpec((A_TILE, HEAD_DIM), lambda b, h, i: (b * n_tiles + i, h)),
        out_shape=jax.ShapeDtypeStruct((t, heads * HEAD_DIM), BF16),
        scratch_shapes=[
            pltpu.VMEM((A_TILE, HEAD_DIM), F32),
            pltpu.VMEM((A_TILE, HEAD_DIM), F32),
            pltpu.VMEM((A_STEP, A_PHASE4, HEAD_DIM), BF16),
            pltpu.VMEM((n16, A_BACK, HEAD_DIM), BF16),
        ] + [
            pltpu.VMEM((A_BACK + A_TILE, HEAD_DIM), BF16),
            pltpu.VMEM((A_STEP, A_BACK + A_PHASE4, HEAD_DIM), BF16),
            pltpu.VMEM((n16, 2 * A_BACK, HEAD_DIM), BF16),
        ] * 2 + [
            pltpu.VMEM((len(A_DILATIONS), A_TILE, HEAD_DIM), F32),
            pltpu.VMEM((len(A_DILATIONS), A_TILE, HEAD_DIM), F32),
            pltpu.VMEM((2, A_BACK, 2 * A_BACK), F32),
        ],
        compiler_params=_params("parallel", "parallel", "arbitrary"),
        name="dilated_attn",
    )(q, k, v)


def _mla_body(q_ref, k_ref, v_ref, o_ref, s_a, s_b, mx_a, mx_b, al_s, m_s, l_s, acc_s, *, tq, tk):
    qi = pl.program_id(2)
    n_tiles = tk // LANES
    n_blocks = 2 * qi + 2
    m_s[...] = jnp.full(m_s.shape, MASK_VALUE, F32)
    l_s[...] = jnp.zeros(l_s.shape, F32)
    acc_s[...] = jnp.zeros(acc_s.shape, F32)

    def key_rows(n):
        return pl.ds(pl.multiple_of(n * tk, tk), tk)

    def logits(n, s_ref, mx_ref, diag_block, r0=0):
        s = _dot_nt(q_ref[0, r0:, :], k_ref[0, key_rows(n), :])
        tiles = [s[:, c * LANES:(c + 1) * LANES] for c in range(n_tiles)]
        if diag_block is not None:
            row = lax.broadcasted_iota(jnp.int32, (tq - r0, LANES), 0) + r0
            col = lax.broadcasted_iota(jnp.int32, (tq - r0, LANES), 1) + diag_block * tk
            tiles = [jnp.where(col + c * LANES <= row, x, MASK_VALUE) for c, x in enumerate(tiles)]
        for c, x in enumerate(tiles):
            s_ref[r0:, c * LANES:(c + 1) * LANES] = x
        mx = jnp.max(functools.reduce(jnp.maximum, tiles), axis=-1, keepdims=True)
        mx_ref[r0:, :] = jnp.broadcast_to(mx, (tq - r0, LANES))

    def softmax_av(s_ref, mx_ref, n, r0=0):
        m_prev = m_s[r0:, :]
        m_new = jnp.maximum(m_prev, mx_ref[r0:, :])
        m_s[r0:, :] = m_new
        al_s[r0:, :] = jnp.exp2(m_prev - m_new)
        pb = jnp.concatenate([jnp.exp2(s_ref[r0:, c * LANES:(c + 1) * LANES] - m_s[r0:, :]).astype(BF16)
                              for c in range(n_tiles)], axis=1)
        v = v_ref[0, key_rows(n), :]
        res = _dot(pb, jnp.concatenate([v, jnp.ones_like(v)], axis=1))
        l_s[r0:, :] = al_s[r0:, :] * l_s[r0:, :] + res[:, LANES:]
        acc_s[r0:, :] = al_s[r0:, :] * acc_s[r0:, :] + res[:, :LANES]

    @pl.when(qi > 0)
    def _():
        logits(0, s_a, mx_a, None)

    def pair(u, carry):
        logits(2 * u + 1, s_b, mx_b, None)
        softmax_av(s_a, mx_a, 2 * u)
        logits(2 * u + 2, s_a, mx_a, None)
        softmax_av(s_b, mx_b, 2 * u + 1)
        return carry

    lax.fori_loop(0, qi - 1, pair, 0)

    @pl.when(qi > 0)
    def _():
        logits(n_blocks - 3, s_b, mx_b, None)
        softmax_av(s_a, mx_a, n_blocks - 4)
        logits(n_blocks - 2, s_a, mx_a, 0)
        softmax_av(s_b, mx_b, n_blocks - 3)

    @pl.when(qi == 0)
    def _():
        logits(0, s_a, mx_a, 0)

    logits(n_blocks - 1, s_b, mx_b, 1, r0=tk)
    softmax_av(s_a, mx_a, n_blocks - 2)
    softmax_av(s_b, mx_b, n_blocks - 1, r0=tk)
    o_ref[...] = (acc_s[...] / l_s[...]).astype(BF16)


def _mla_attn(q, k, v, *, batch, seq, tq, tk):
    assert tq == 2 * tk
    heads, t, _ = q.shape
    nq = seq // tq
    kv = pl.BlockSpec((1, seq, LANES), lambda h, b, i: (h, b, 0))
    return pl.pallas_call(
        functools.partial(_mla_body, tq=tq, tk=tk),
        grid=(heads, batch, nq),
        in_specs=[pl.BlockSpec((1, tq, LANES), lambda h, b, i: (h, b * nq + i, 0)), kv, kv],
        out_specs=pl.BlockSpec((tq, LANES), lambda h, b, i: (b * nq + i, h)),
        out_shape=jax.ShapeDtypeStruct((t, heads * LANES), BF16),
        scratch_shapes=[pltpu.VMEM((tq, tk), F32)] * 2 + [pltpu.VMEM((tq, LANES), F32)] * 6,
        compiler_params=_params("parallel", "parallel", "arbitrary"),
        name="mla_attn",
    )(q, k, v)


def _odd_proj_body(h_ref, g_ref, win_ref, pw_ref, ps_ref, oc_ref, qd_ref, kd_ref, vd_ref, ext, *, tm, tiles_per_seq):
    i = pl.program_id(0)
    proj, = _rms_dot(h_ref[...], g_ref[...], win_ref[...])

    @pl.when(i % tiles_per_seq == 0)
    def _():
        ext[0:C_HALO, :] = jnp.zeros((C_HALO, C_WIDTH), F32)

    ext[C_HALO:C_HALO + tm, :] = proj[:, 0:C_WIDTH]
    t_in_seq = (i % tiles_per_seq) * tm + lax.broadcasted_iota(jnp.int32, (tm, 1), 0)
    for g, w in enumerate(C_WINDOWS):
        cols = slice(g * C_GROUP_DIM, (g + 1) * C_GROUP_DIM)
        s = ext[C_HALO - (w - 1):C_HALO + tm, cols]
        step = 1
        while step < w:
            s = s[step:] + s[:-step]
            step *= 2
        count = jnp.minimum(t_in_seq + 1, w).astype(F32)
        pooled = s / count - proj[:, cols]
        oc_ref[:, cols] = (_dot(pooled.astype(BF16), pw_ref[g]) * ps_ref[:, cols]).astype(BF16)
    ext[0:C_HALO, :] = ext[tm:tm + C_HALO, :]

    dw = D_HEADS * HEAD_DIM
    for hd in range(D_HEADS):
        lanes = lambda base: slice(base + hd * LANES, base + (hd + 1) * LANES)
        qd_ref[hd] = (proj[:, lanes(C_WIDTH)] * (HEAD_DIM ** -0.5 * LOG2_E)).astype(BF16)
        kd_ref[hd] = proj[:, lanes(C_WIDTH + dw)].astype(BF16)
        vd_ref[hd] = proj[:, lanes(C_WIDTH + 2 * dw)].astype(BF16)


def _odd_proj(h, g, w_in, pool_w, pool_scale, *, seq, tm):
    t, d = h.shape
    row = lambda w: pl.BlockSpec((tm, w), lambda i: (i, 0))
    heads = pl.BlockSpec((D_HEADS, tm, LANES), lambda i: (0, i, 0))
    shp = jax.ShapeDtypeStruct((D_HEADS, t, LANES), BF16)
    return pl.pallas_call(
        functools.partial(_odd_proj_body, tm=tm, tiles_per_seq=seq // tm),
        grid=(t // tm,),
        in_specs=[row(d), _resident((1, d)), _resident(w_in.shape), _resident(pool_w.shape), _resident(pool_scale.shape)],
        out_specs=[row(C_WIDTH), heads, heads, heads],
        out_shape=[jax.ShapeDtypeStruct((t, C_WIDTH), BF16), shp, shp, shp],
        scratch_shapes=[pltpu.VMEM((C_HALO + tm, C_WIDTH), F32)],
        compiler_params=_params("arbitrary"),
        name="odd_proj",
    )(h, g, w_in, pool_w, pool_scale)


SB_SUB = 256
SB_DEAD_CARRY = 160.0


def _sb_body(q_ref, k_ref, v_ref, o_ref, z_a, z_b, e_a, e_b, carry_s, acc_s, *, tq, tk, heads):
    qi = pl.program_id(2)
    n_blocks = 2 * qi + 2
    carry_s[...] = jnp.zeros(carry_s.shape, F32)
    acc_s[...] = jnp.zeros(acc_s.shape, F32)
    later = (lax.broadcasted_iota(jnp.int32, (SB_SUB, SB_SUB), 0) > lax.broadcasted_iota(jnp.int32, (SB_SUB, SB_SUB), 1))
    later = jnp.where(later, 1.0, 0.0).astype(BF16)

    def key_rows(n):
        kb = jnp.maximum(n_blocks - 1 - n, 0)
        return pl.ds(pl.multiple_of(kb * tk, tk), tk)

    def logits(n, z_ref, r0=0):
        for h in range(heads):
            z_ref[h, r0:, :] = _dot_nt(q_ref[h, r0:, :], k_ref[h, key_rows(n), :])

    def gate(z_ref, e_ref, diag_block, r0=0):
        for h in range(heads):
            carry = carry_s[h, r0:, :]
            for sub in reversed(range(tk // SB_SUB)):
                z = z_ref[h, r0:, sub * SB_SUB:(sub + 1) * SB_SUB]
                softplus = jnp.maximum(z, 0.0) + jnp.log2(1.0 + jnp.exp2(-jnp.abs(z)))
                arg = z - softplus
                if diag_block is None:
                    keep = softplus
                else:
                    row = lax.broadcasted_iota(jnp.int32, (tq - r0, SB_SUB), 0) + r0
                    col = lax.broadcasted_iota(jnp.int32, (tq - r0, SB_SUB), 1) + diag_block * tk + sub * SB_SUB
                    keep = jnp.where(col < row, softplus, 0.0)
                    arg = jnp.where(col < row, arg, MASK_VALUE)
                expo = arg - _dot(keep.astype(BF16), later)
                for c in range(SB_SUB // LANES):
                    lo = sub * SB_SUB + c * LANES
                    e_ref[h, r0:, lo:lo + LANES] = expo[:, c * LANES:(c + 1) * LANES] - carry
                carry = carry + jnp.sum(keep, axis=-1, keepdims=True)
            carry_s[h, r0:, :] = carry

    def weigh(e_ref, n, r0=0):
        for h in range(heads):
            acc_s[h, r0:, :] += _dot(jnp.exp2(e_ref[h, r0:, :]).astype(BF16), v_ref[h, key_rows(n), :])

    logits(0, z_a, r0=tk)
    logits(1, z_b)
    gate(z_a, e_a, 1, r0=tk)
    logits(2, z_a)
    gate(z_b, e_b, 0)
    weigh(e_a, 0, r0=tk)

    def step(n, z_this, e_this, z_next, e_prev):
        logits(n + 1, z_next)
        gate(z_this, e_this, None)
        weigh(e_prev, n - 1)

    def one_block(state):
        n, _ = state

        @pl.when(n % 2 == 0)
        def _():
            step(n, z_a, e_a, z_b, e_b)

        @pl.when(n % 2 == 1)
        def _():
            step(n, z_b, e_b, z_a, e_a)

        return n + 1, jnp.min(carry_s[...])

    def more_blocks(state):
        n, min_carry = state
        return (n < n_blocks) & (min_carry < SB_DEAD_CARRY)

    n_done, _ = lax.while_loop(more_blocks, one_block, (jnp.int32(2), jnp.min(carry_s[...])))

    @pl.when(n_done % 2 == 0)
    def _():
        weigh(e_b, n_done - 1)

    @pl.when(n_done % 2 == 1)
    def _():
        weigh(e_a, n_done - 1)

    for h in range(heads):
        o_ref[:, h * LANES:(h + 1) * LANES] = acc_s[h].astype(BF16)


def _sb_attn(q, k, v, *, batch, seq, tq, tk, heads_per_step):
    assert tq == 2 * tk and tk % SB_SUB == 0
    heads, t, _ = q.shape
    hp = heads_per_step
    nq = seq // tq
    kv = pl.BlockSpec((hp, seq, LANES), lambda h, b, i: (h, b, 0), pipeline_mode=pl.Buffered(1))
    return pl.pallas_call(
        functools.partial(_sb_body, tq=tq, tk=tk, heads=hp),
        grid=(heads // hp, batch, nq),
        in_specs=[pl.BlockSpec((hp, tq, LANES), lambda h, b, i: (h, b * nq + i, 0)), kv, kv],
        out_specs=pl.BlockSpec((tq, hp * LANES), lambda h, b, i: (b * nq + i, h)),
        out_shape=jax.ShapeDtypeStruct((t, heads * LANES), BF16),
        scratch_shapes=[pltpu.VMEM((hp, tq, tk), F32)] * 4 + [pltpu.VMEM((hp, tq, LANES), F32)] * 2,
        compiler_params=_params("parallel", "parallel", "arbitrary"),
        name="sb_attn",
    )(q, k, v)


def _latent_head_lanes():
    half_pe, half_nope = B_ROPE // 2, B_NOPE // 2
    pe = np.concatenate([np.arange(half_pe), LANES // 2 + np.arange(half_pe)])
    nope = np.concatenate([half_pe + np.arange(half_nope), LANES // 2 + half_pe + np.arange(half_nope)])
    return pe, nope


def _scatter_cols(w, src_cols, dst_cols, width):
    out = jnp.zeros((w.shape[0], width), w.dtype)
    return out.at[:, np.asarray(dst_cols)].set(w[:, np.asarray(src_cols)])


def _even_weights(w_in, w_q_up, w_kv_up):
    pe, nope = _latent_head_lanes()
    a_w = A_HEADS * HEAD_DIM
    o_kr = 3 * a_w + B_Q_LORA + B_KV_LORA
    k_rope = _scatter_cols(w_in, o_kr + np.arange(B_ROPE), pe, LANES)
    w_in_p = jnp.concatenate([w_in[:, :o_kr], k_rope], axis=1).astype(BF16)
    qd, kvd = B_NOPE + B_ROPE, B_NOPE + B_V
    src_q, dst_q, src_k, dst_k, src_v = [], [], [], [], []
    for hd in range(B_HEADS):
        src_q += list(hd * qd + np.arange(B_NOPE)) + list(hd * qd + B_NOPE + np.arange(B_ROPE))
        dst_q += list(hd * LANES + nope) + list(hd * LANES + pe)
        src_k += list(hd * kvd + np.arange(B_NOPE))
        dst_k += list(hd * LANES + nope)
        src_v += list(hd * kvd + B_NOPE + np.arange(B_V))
    w_q = _scatter_cols(w_q_up, src_q, dst_q, B_HEADS * LANES).astype(BF16)
    w_k = _scatter_cols(w_kv_up, src_k, dst_k, B_HEADS * LANES)
    w_kv = jnp.concatenate([w_k, w_kv_up[:, np.asarray(src_v)]], axis=1).astype(BF16)
    return w_in_p, w_q, w_kv


def _rope_frequencies():
    inv_a = ROPE_THETA ** (-jnp.arange(0, HEAD_DIM, 2, dtype=F32) / HEAD_DIM)
    inv_b = ROPE_THETA ** (-jnp.arange(0, B_ROPE, 2, dtype=F32) / B_ROPE)
    pad = jnp.zeros((LANES - inv_a.shape[0] - inv_b.shape[0],), F32)
    return jnp.concatenate([inv_a, inv_b, pad])[None, :]


def kernel(x, positions, norm_g, ffn_w_gate, ffn_w_up, ffn_w_down, even_w_in, even_q_norm, even_w_q_up, even_kv_norm, even_w_kv_up, even_w_out, odd_w_in, odd_pool_w, odd_pool_scale, odd_w_out, final_norm):
    batch, seq, d = x.shape
    t = batch * seq
    assert seq % A_TILE == 0
    tm = 512
    h = x.reshape(t, d).astype(F32)
    pos = positions.reshape(t, 1).astype(jnp.int32)
    bf = lambda w: w.astype(BF16)
    gain = lambda i, j: norm_g[i, j][None, :].astype(F32)
    ffn = functools.partial(_ffn, wg=ffn_w_gate.astype(F32), wu=ffn_w_up.astype(F32), wd=ffn_w_down.astype(F32), tm=tm)

    h = ffn(h, [], None, gain(0, 0), which=(0, 0))
    w_in_p, w_q, w_kv = _even_weights(even_w_in[0], even_w_q_up[0], even_w_kv_up[0])
    qa, ka, va, qb, kb, vb = _even_proj(h, gain(0, 1), w_in_p, pos, _rope_frequencies(), even_q_norm[0][None, :].astype(F32), w_q,
                                        even_kv_norm[0][None, :].astype(F32), w_kv, tm=tm)
    out_a = _dilated_attn(qa, ka, va, batch=batch, seq=seq)
    out_b = _mla_attn(qb, kb, vb, batch=batch, seq=seq, tq=1024, tk=512)
    h = ffn(h, [out_a, out_b], bf(even_w_out[0]), gain(0, 2), which=(0, 1))

    h = ffn(h, [], None, gain(1, 0), which=(1, 0))
    out_c, qd, kd, vd = _odd_proj(h, gain(1, 1), bf(odd_w_in[0]), bf(odd_pool_w[0]),
                                  odd_pool_scale[0][None, :].astype(F32), seq=seq, tm=tm)
    out_d = _sb_attn(qd, kd, vd, batch=batch, seq=seq, tq=512, tk=256, heads_per_step=4)
    h = ffn(h, [out_c, out_d], bf(odd_w_out[0]), gain(1, 2), which=(1, 1), final_g=final_norm[None, :].astype(F32))
    return h.reshape(batch, seq, d).astype(x.dtype)
```
